```python
import jax, jax.numpy as jnp
from jax import lax
import numpy as np


D_MODEL = 1024
BATCH = 32
SEQ = 2048
DEPTH = 1
DEC_BATCH = 8
DEC_SEQ = 4096
PAST_LEN = 128

HEAD_DIM = 64
ROT_DIM = HEAD_DIM // 4
ROPE_THETA = 500000.0
A_HEADS = 8
A_KV_HEADS = 2
A_GROUPS = A_HEADS // A_KV_HEADS
WINDOW = 128
A_BLOCK = 128
B_HEADS = 8
GRID_W = 64
NA_ROWS_MAX = 8
NA_COLS = 16
NA_COL_BLOCK = NA_COLS
NA_COL_SPAN = 2 * NA_COLS
N_EXPERTS = 32
TOP_K = 4
D_FF = D_MODEL
SWIGLU_LIMIT = 7.0
SWIGLU_ALPHA = 1.702
RMS_EPS = 1e-6
NEG = -1e30
A_Q = A_HEADS * HEAD_DIM
A_KV = A_KV_HEADS * HEAD_DIM
B_W = B_HEADS * HEAD_DIM
IN_WIDTH = A_Q + 2 * A_KV + 3 * B_W + 2 * D_MODEL

kernel_name = 'hybrid_window_gqa_natten_moe_encoder'


def rms_norm(x, g):
    xf = x.astype(jnp.float32)
    y = xf * lax.rsqrt(jnp.mean(xf * xf, axis=-1, keepdims=True) + RMS_EPS)
    return (y * g.astype(jnp.float32)).astype(x.dtype)


def partial_rope(x, pos):
    half = ROT_DIM // 2
    inv = jnp.power(jnp.float32(ROPE_THETA), -jnp.arange(0, ROT_DIM, 2, dtype=jnp.float32) / ROT_DIM)
    ang = pos.astype(jnp.float32)[:, None] * inv[None, :]
    cos = jnp.cos(ang)[None, :, None, :]
    sin = jnp.sin(ang)[None, :, None, :]
    xf = x.astype(jnp.float32)
    x1 = xf[..., :half]
    x2 = xf[..., half:ROT_DIM]
    out = jnp.concatenate([x1 * cos - x2 * sin, x2 * cos + x1 * sin, xf[..., ROT_DIM:]], axis=-1)
    return out.astype(x.dtype)


def window_attention(q, k, v, sink):
    bn, s_len = q.shape[0], q.shape[1]
    nb = s_len // A_BLOCK
    span = A_BLOCK + 2 * WINDOW
    qb = q.reshape(bn, nb, A_BLOCK, A_KV_HEADS, A_GROUPS, HEAD_DIM).swapaxes(0, 1)
    pad = ((0, 0), (WINDOW, WINDOW), (0, 0), (0, 0))
    kp = jnp.pad(k, pad)
    vp = jnp.pad(v, pad)
    qi = jnp.arange(A_BLOCK)[:, None]
    kj = jnp.arange(span)[None, :]
    band = jnp.abs(qi - kj + WINDOW) <= WINDOW
    sink_f = sink.astype(jnp.float32).reshape(A_KV_HEADS, A_GROUPS)[..., None]
    scale = HEAD_DIM ** -0.5

    def block(args):
        n, q_blk = args
        start = n * A_BLOCK
        kb = lax.dynamic_slice_in_dim(kp, start, span, axis=1)
        vb = lax.dynamic_slice_in_dim(vp, start, span, axis=1)
        kpos = start - WINDOW + jnp.arange(span)
        mask = band & ((kpos >= 0) & (kpos < s_len))[None, :]
        s = jnp.einsum('bqkgd,bjkd->bkgqj', q_blk, kb, preferred_element_type=jnp.float32) * scale
        s = jnp.where(mask, s, NEG)
        lse = jnp.logaddexp(jax.nn.logsumexp(s, axis=-1), sink_f)
        p = jnp.exp(s - lse[..., None])
        return jnp.einsum('bkgqj,bjkd->bqkgd', p.astype(v.dtype), vb)

    out = lax.map(block, (jnp.arange(nb), qb))
    return out.swapaxes(0, 1).reshape(bn, s_len, A_Q)


def neighbourhood_attention(q, k, v, rpb):
    bn, s_len = q.shape[0], q.shape[1]
    rows = s_len // GRID_W
    kh = min(NA_ROWS_MAX, rows)
    ncb = GRID_W // NA_COL_BLOCK
    qg = q.reshape(bn, rows, ncb, NA_COL_BLOCK, B_HEADS, HEAD_DIM).swapaxes(0, 1)
    kg = k.reshape(bn, rows, GRID_W, B_HEADS, HEAD_DIM)
    vg = v.reshape(bn, rows, GRID_W, B_HEADS, HEAD_DIM)
    qcol = np.arange(GRID_W).reshape(ncb, NA_COL_BLOCK)
    cstart = np.clip(np.arange(ncb) * NA_COL_BLOCK - NA_COLS // 2, 0, GRID_W - NA_COL_SPAN)
    kcol = cstart[:, None] + np.arange(NA_COL_SPAN)
    qstart = np.clip(qcol - NA_COLS // 2, 0, GRID_W - NA_COLS)
    cmask = (kcol[:, None, :] >= qstart[..., None]) & (kcol[:, None, :] < qstart[..., None] + NA_COLS)
    cidx = np.clip(kcol[:, None, :] - qcol[..., None] + NA_COLS - 1, 0, 2 * NA_COLS - 2)
    bias_c = rpb.astype(jnp.float32)[:, :, cidx]
    cmask_b = jnp.asarray(cmask)[:, :, None, :]
    scale = HEAD_DIM ** -0.5

    def row(args):
        r, q_row = args
        r0 = jnp.clip(r - kh // 2, 0, rows - kh)
        k_rows = lax.dynamic_slice_in_dim(kg, r0, kh, axis=1)
        v_rows = lax.dynamic_slice_in_dim(vg, r0, kh, axis=1)
        k_blk = k_rows[:, :, kcol]
        v_blk = v_rows[:, :, kcol]
        roff = r0 + jnp.arange(kh) - r + NA_ROWS_MAX - 1
        bias = jnp.take(bias_c, roff, axis=1).transpose(0, 2, 3, 1, 4)
        s = jnp.einsum('bcqhd,bacxhd->bhcqax', q_row, k_blk, preferred_element_type=jnp.float32) * scale
        s = jnp.where(cmask_b, s + bias[None], NEG)
        shp = s.shape
        p = jax.nn.softmax(s.reshape(shp[0], shp[1], shp[2], shp[3], -1), axis=-1).reshape(shp)
        o = jnp.einsum('bhcqax,bacxhd->bcqhd', p.astype(v.dtype), v_blk)
        return o.reshape(bn, GRID_W, B_HEADS, HEAD_DIM)

    out = lax.map(row, (jnp.arange(rows), qg))
    return out.swapaxes(0, 1).reshape(bn, s_len, B_W)


def moe_ffn(h, w_router, b_router, w_gate, b_gate, w_up, b_up, w_down, b_down):
    bn, s_len, d = h.shape
    xt = h.reshape(-1, d)
    logits = (xt @ w_router).astype(jnp.float32) + b_router.astype(jnp.float32)
    top_v, top_i = lax.top_k(logits, TOP_K)
    top_w = jax.nn.softmax(top_v, axis=-1)
    combine = jnp.sum(jax.nn.one_hot(top_i, N_EXPERTS, dtype=jnp.float32) * top_w[..., None], axis=1).astype(h.dtype)
    out = jnp.zeros_like(xt)
    for e in range(N_EXPERTS):
        g = jnp.minimum(xt @ w_gate[e] + b_gate[e], SWIGLU_LIMIT)
        u = jnp.clip(xt @ w_up[e] + b_up[e], -SWIGLU_LIMIT, SWIGLU_LIMIT)
        act = (u + 1) * (g * jax.nn.sigmoid(SWIGLU_ALPHA * g))
        out = out + combine[:, e:e + 1] * (act @ w_down[e] + b_down[e])
    return out.reshape(bn, s_len, d)


def encoder_layer(x, c, ada_w, ada_b, g_pre_mix, g_post_mix, g_pre_ffn, g_post_ffn, w_in, sink_a, rpb_b,
                  w_branch_a, w_branch_b, w_out, w_router, b_router, w_gate, b_gate, w_up, b_up, w_down, b_down):
    bn, s_len, _ = x.shape
    mod = (jax.nn.silu(c) @ ada_w + ada_b)[:, None, :]
    sh1, sc1, gt1, sh2, sc2, gt2 = jnp.split(mod, 6, axis=-1)
    h = rms_norm(x, g_pre_mix) * (1 + sc1) + sh1
    proj = h @ w_in
    cuts = np.cumsum([A_Q, A_KV, A_KV, B_W, B_W, B_W, D_MODEL])
    qa, ka, va, qb, kb, vb, ga, gb = jnp.split(proj, [int(t) for t in cuts], axis=-1)
    pos = jnp.arange(s_len)
    qa = partial_rope(qa.reshape(bn, s_len, A_HEADS, HEAD_DIM), pos)
    ka = partial_rope(ka.reshape(bn, s_len, A_KV_HEADS, HEAD_DIM), pos)
    va = va.reshape(bn, s_len, A_KV_HEADS, HEAD_DIM)
    ya = window_attention(qa, ka, va, sink_a) @ w_branch_a
    hb = (bn, s_len, B_HEADS, HEAD_DIM)
    yb = neighbourhood_attention(qb.reshape(hb), kb.reshape(hb), vb.reshape(hb), rpb_b) @ w_branch_b
    merged = jax.nn.sigmoid(ga) * ya + jax.nn.sigmoid(gb) * yb
    x = x + gt1 * rms_norm(merged @ w_out, g_post_mix)
    h2 = rms_norm(x, g_pre_ffn) * (1 + sc2) + sh2
    y2 = moe_ffn(h2, w_router, b_router, w_gate, b_gate, w_up, b_up, w_down, b_down)
    return x + gt2 * rms_norm(y2, g_post_ffn)


def setup_inputs(seed: int = 0) -> dict:
    key = jax.random.key(seed)
    ks = jax.random.split(key, 32)

    def nrm(k, shape, scale):
        return jax.random.normal(k, shape, jnp.float32) * scale

    L, D, E, F = DEPTH, D_MODEL, N_EXPERTS, D_FF
    return {
        'x_prompt': nrm(ks[0], (BATCH, SEQ, D), 1.0),
        'x_sample': nrm(ks[1], (DEC_BATCH, DEC_SEQ, D), 1.0),
        'c_prompt': nrm(ks[2], (BATCH, D), 1.0),
        'c_sample': nrm(ks[3], (DEC_BATCH, D), 1.0),
        'ada_w': nrm(ks[4], (L, D, 6 * D), D ** -0.5),
        'ada_b': nrm(ks[5], (L, 6 * D), 0.02),
        'g_pre_mix': 1.0 + nrm(ks[6], (L, D), 0.02),
        'g_post_mix': 1.0 + nrm(ks[7], (L, D), 0.02),
        'g_pre_ffn': 1.0 + nrm(ks[8], (L, D), 0.02),
        'g_post_ffn': 1.0 + nrm(ks[9], (L, D), 0.02),
        'w_in': nrm(ks[10], (L, D, IN_WIDTH), D ** -0.5),
        'sink_a': nrm(ks[11], (L, A_HEADS), 0.5),
        'rpb_b': nrm(ks[12], (L, B_HEADS, 2 * NA_ROWS_MAX - 1, 2 * NA_COLS - 1), 0.1),
        'w_branch_a': nrm(ks[13], (L, A_Q, D), A_Q ** -0.5),
        'w_branch_b': nrm(ks[14], (L, B_W, D), B_W ** -0.5),
        'w_out': nrm(ks[15], (L, D, D), D ** -0.5),
        'w_router': nrm(ks[16], (L, D, E), D ** -0.5),
        'b_router': nrm(ks[17], (L, E), 0.01),
        'w_gate': nrm(ks[18], (L, E, D, F), D ** -0.5),
        'b_gate': nrm(ks[19], (L, E, F), 0.02),
        'w_up': nrm(ks[20], (L, E, D, F), D ** -0.5),
        'b_up': nrm(ks[21], (L, E, F), 0.02),
        'w_down': nrm(ks[22], (L, E, F, D), F ** -0.5),
        'b_down': nrm(ks[23], (L, E, D), 0.02),
    }


def reference(x_prompt, x_sample, c_prompt, c_sample, ada_w, ada_b, g_pre_mix, g_post_mix, g_pre_ffn, g_post_ffn,
              w_in, sink_a, rpb_b, w_branch_a, w_branch_b, w_out, w_router, b_router,
              w_gate, b_gate, w_up, b_up, w_down, b_down):
    def trunk(x, c):
        for l in range(DEPTH):
            x = encoder_layer(x, c, ada_w[l], ada_b[l], g_pre_mix[l], g_post_mix[l], g_pre_ffn[l], g_post_ffn[l],
                              w_in[l], sink_a[l], rpb_b[l], w_branch_a[l], w_branch_b[l], w_out[l],
                              w_router[l], b_router[l], w_gate[l], b_gate[l], w_up[l], b_up[l],
                              w_down[l], b_down[l])
        return x

    y_prompt = trunk(x_prompt, c_prompt)
    y_sample = trunk(x_sample, c_sample)
    return (y_prompt, y_sample)
```

```python
import functools

import jax
import jax.numpy as jnp
import numpy as np
from jax import lax
from jax.experimental import pallas as pl
from jax.experimental.pallas import tpu as pltpu

D_MODEL = 1024
HEAD_DIM = 64
ROT_DIM = HEAD_DIM // 4
ROPE_THETA = 500000.0
A_HEADS = 8
A_KV_HEADS = 2
WINDOW = 128
A_BLOCK = 128
B_HEADS = 8
GRID_W = 64
NA_ROWS = 8
NA_COLS = 16
N_EXPERTS = 32
TOP_K = 4
SWIGLU_LIMIT = 7.0
SWIGLU_ALPHA = 1.702
RMS_EPS = 1e-6
NEG = -1e30
A_Q = A_HEADS * HEAD_DIM
A_KV = A_KV_HEADS * HEAD_DIM
B_W = B_HEADS * HEAD_DIM

LANES = 128
PAIR = 2 * HEAD_DIM
N_PAIRS = B_W // PAIR
MOE_TILE = 512
ROW_BLOCK = 1024
VMEM_LIMIT = 56 * 1024 * 1024

BF16 = jnp.bfloat16
F32 = jnp.float32


def _cparams(*sem):
    return pltpu.CompilerParams(dimension_semantics=sem, vmem_limit_bytes=VMEM_LIMIT)


def _mod_body(c_ref, w_ref, b_ref, o_ref):
    c = c_ref[...]
    s = c * (1.0 / (1.0 + jnp.exp(-c)))
    o_ref[...] = jnp.dot(s, w_ref[...], preferred_element_type=F32,
                         precision=lax.Precision.HIGHEST) + b_ref[...]


def _modulation(c, ada_w, ada_b):
    n, d = c.shape
    blk = 1024
    return pl.pallas_call(
        _mod_body,
        grid=(ada_w.shape[1] // blk,),
        in_specs=[pl.BlockSpec((n, d), lambda j: (0, 0)),
                  pl.BlockSpec((d, blk), lambda j: (0, j)),
                  pl.BlockSpec((1, blk), lambda j: (0, j))],
        out_specs=pl.BlockSpec((n, blk), lambda j: (0, j)),
        out_shape=jax.ShapeDtypeStruct((n, ada_w.shape[1]), F32),
        compiler_params=_cparams("arbitrary"),
        name="adaln_mod",
    )(c, ada_w, ada_b.reshape(1, -1))


def _rms(x, g):
    return x * lax.rsqrt(jnp.mean(x * x, axis=-1, keepdims=True) + RMS_EPS) * g


def _rope_tile(seg, rope_ref):
    return (seg * rope_ref[0] + pltpu.roll(seg, LANES - ROT_DIM // 2, 1) * rope_ref[1]
            + pltpu.roll(seg, ROT_DIM // 2, 1) * rope_ref[2])


def _inproj_body(x_ref, mod_ref, g_ref, w_ref, rope_ref,
                 qa_ref, ka_ref, va_ref, qb_ref, kb_ref, vb_ref, ga_ref, gb_ref):
    sh1 = mod_ref[0, 0:1, :]
    sc1 = mod_ref[0, 1:2, :]
    h = (_rms(x_ref[0], g_ref[...]) * (1.0 + sc1) + sh1).astype(BF16)

    def proj(lo, hi):
        return jnp.dot(h, w_ref[:, lo:hi], preferred_element_type=F32)

    o = 0
    for t in range(A_Q // LANES):
        qa_ref[0, :, t * LANES:(t + 1) * LANES] = _rope_tile(
            proj(o + t * LANES, o + (t + 1) * LANES), rope_ref).astype(BF16)
    o += A_Q
    ka_ref[0] = _rope_tile(proj(o, o + A_KV), rope_ref).astype(BF16)
    o += A_KV
    va_ref[0] = proj(o, o + A_KV).astype(BF16)
    o += A_KV
    for ref in (qb_ref, kb_ref, vb_ref):
        ref[0] = proj(o, o + B_W).astype(BF16)
        o += B_W
    for ref in (ga_ref, gb_ref):
        ref[0] = proj(o, o + D_MODEL).astype(BF16)
        o += D_MODEL


def _in_projection(x, mod, g_pre, w_in_bf, rope, tm):
    b, s, d = x.shape
    widths = (A_Q, A_KV, A_KV, B_W, B_W, B_W, D_MODEL, D_MODEL)
    tok = lambda w: pl.BlockSpec((1, tm, w), lambda i, j: (i, j, 0))
    return pl.pallas_call(
        _inproj_body,
        grid=(b, s // tm),
        in_specs=[tok(d),
                  pl.BlockSpec((1, 6, d), lambda i, j: (i, 0, 0)),
                  pl.BlockSpec((1, d), lambda i, j: (0, 0)),
                  pl.BlockSpec(w_in_bf.shape, lambda i, j: (0, 0)),
                  pl.BlockSpec((3, tm, LANES), lambda i, j: (0, j, 0))],
        out_specs=[tok(w) for w in widths],
        out_shape=[jax.ShapeDtypeStruct((b, s, w), BF16) for w in widths],
        compiler_params=_cparams("arbitrary", "arbitrary"),
        name="in_projection",
    )(x, mod, g_pre.reshape(1, d), w_in_bf, rope)


def _stack_heads(t):
    lane = lax.broadcasted_iota(jnp.int32, t.shape, 1)
    zero = jnp.zeros_like(t)
    return jnp.concatenate([jnp.where(lane < HEAD_DIM, t, zero),
                            jnp.where(lane >= HEAD_DIM, t, zero)], axis=0)


def _window_body(q_ref, k_ref, v_ref, sink_ref, o_ref, *, seq, tq):
    span = A_BLOCK + 2 * WINDOW
    n_sub = tq // A_BLOCK
    rows = N_PAIRS * A_BLOCK
    qi = lax.broadcasted_iota(jnp.int32, (rows, 2 * span), 0) % A_BLOCK
    kj = lax.broadcasted_iota(jnp.int32, (rows, 2 * span), 1) % span
    lane_o = lax.broadcasted_iota(jnp.int32, (rows, LANES), 1)
    sink = sink_ref[...]
    for sub in range(n_sub):
        q0 = pl.program_id(1) * tq + sub * A_BLOCK
        k0 = pl.multiple_of(jnp.clip(q0 - WINDOW, 0, seq - span), A_BLOCK)
        kst = _stack_heads(k_ref[0, pl.ds(k0, span), :])
        vst = _stack_heads(v_ref[0, pl.ds(k0, span), :])
        qblk = q_ref[0, sub * A_BLOCK:(sub + 1) * A_BLOCK, :]
        qs = jnp.concatenate([qblk[:, p * LANES:(p + 1) * LANES] for p in range(N_PAIRS)], axis=0)
        s = lax.dot_general(qs, kst, (((1,), (1,)), ((), ())), preferred_element_type=F32)
        band = jnp.abs((q0 + qi) - (k0 + kj)) <= WINDOW
        s = jnp.where(band, s, NEG)
        outs = []
        ps = []
        for half in range(A_KV_HEADS):
            sh = s[:, half * span:(half + 1) * span]
            snk = sink[:, half:half + 1]
            m = jnp.maximum(jnp.max(sh, axis=-1, keepdims=True), snk)
            e = jnp.exp(sh - m)
            den = jnp.sum(e, axis=-1, keepdims=True) + jnp.exp(snk - m)
            ps.append(e.astype(BF16))
            outs.append(1.0 / den)
        p = jnp.concatenate(ps, axis=1)
        o = jnp.dot(p, vst, preferred_element_type=F32)
        o = o * jnp.where(lane_o < HEAD_DIM, outs[0], outs[1])
        for pr in range(N_PAIRS):
            o_ref[0, sub * A_BLOCK:(sub + 1) * A_BLOCK, pr * LANES:(pr + 1) * LANES] = (
                o[pr * A_BLOCK:(pr + 1) * A_BLOCK, :].astype(BF16))


def _window_attention(q, k, v, sink_rows, tq):
    b, s, _ = q.shape
    return pl.pallas_call(
        functools.partial(_window_body, seq=s, tq=tq),
        grid=(b, s // tq),
        in_specs=[pl.BlockSpec((1, tq, A_Q), lambda i, j: (i, j, 0)),
                  pl.BlockSpec((1, s, A_KV), lambda i, j: (i, 0, 0)),
                  pl.BlockSpec((1, s, A_KV), lambda i, j: (i, 0, 0)),
                  pl.BlockSpec(sink_rows.shape, lambda i, j: (0, 0))],
        out_specs=pl.BlockSpec((1, tq, A_Q), lambda i, j: (i, j, 0)),
        out_shape=jax.ShapeDtypeStruct((b, s, A_Q), BF16),
        compiler_params=_cparams("arbitrary", "arbitrary"),
        name="window_attention",
    )(q, k, v, sink_rows)


def _nbr_bias_body(rpb_ref, o_ref):
    p = pl.program_id(0)
    d = pl.program_id(1)
    nkeys = NA_ROWS * GRID_W
    qc = lax.broadcasted_iota(jnp.int32, (GRID_W, 2 * nkeys), 0)
    col = lax.broadcasted_iota(jnp.int32, (GRID_W, 2 * nkeys), 1)
    kc = col % GRID_W
    qstart = jnp.clip(qc - NA_COLS // 2, 0, GRID_W - NA_COLS)
    valid = (kc >= qstart) & (kc < qstart + NA_COLS)
    cidx = kc - qc + NA_COLS - 1
    col1 = lax.broadcasted_iota(jnp.int32, (1, 2 * nkeys), 1)
    blk1 = col1 // GRID_W
    acc = jnp.full((GRID_W, 2 * nkeys), NEG, F32)
    for j in range(2 * NA_COLS - 1):
        rv = jnp.zeros((1, 2 * nkeys), F32)
        for hh in range(2):
            for aa in range(NA_ROWS):
                rv = jnp.where(blk1 == hh * NA_ROWS + aa,
                               rpb_ref[2 * p + hh, aa - d + NA_ROWS - 1, j], rv)
        acc = jnp.where(valid & (cidx == j), rv, acc)
    o_ref[0, 0] = acc


def _nbr_bias(rpb):
    nkeys = NA_ROWS * GRID_W
    return pl.pallas_call(
        _nbr_bias_body,
        grid=(N_PAIRS, NA_ROWS),
        in_specs=[pl.BlockSpec(memory_space=pltpu.SMEM)],
        out_specs=pl.BlockSpec((1, 1, GRID_W, 2 * nkeys), lambda p, d: (p, d, 0, 0)),
        out_shape=jax.ShapeDtypeStruct((N_PAIRS, NA_ROWS, GRID_W, 2 * nkeys), F32),
        compiler_params=_cparams("arbitrary", "arbitrary"),
        name="nbr_bias",
    )(rpb)


def _nbr_body(q_ref, k_ref, v_ref, bias_ref, o_ref, *, seq, tq):
    n_rows = seq // GRID_W
    nkeys = NA_ROWS * GRID_W
    lane_o = lax.broadcasted_iota(jnp.int32, (GRID_W, LANES), 1)

    def row(rl, carry):
        r = pl.program_id(1) * (tq // GRID_W) + rl
        r0 = jnp.clip(r - NA_ROWS // 2, 0, n_rows - NA_ROWS)
        d = r - r0
        k0 = pl.multiple_of(r0 * GRID_W, GRID_W)
        q0 = pl.multiple_of(rl * GRID_W, GRID_W)
        for pr in range(N_PAIRS):
            lanes = slice(pr * LANES, (pr + 1) * LANES)
            kst = _stack_heads(k_ref[0, pl.ds(k0, nkeys), lanes])
            vst = _stack_heads(v_ref[0, pl.ds(k0, nkeys), lanes])
            qp = q_ref[0, pl.ds(q0, GRID_W), lanes]
            s = lax.dot_general(qp, kst, (((1,), (1,)), ((), ())), preferred_element_type=F32)
            s = s + bias_ref[pr, d]
            ps = []
            inv = []
            for half in range(2):
                sh = s[:, half * nkeys:(half + 1) * nkeys]
                m = jnp.max(sh, axis=-1, keepdims=True)
                e = jnp.exp(sh - m)
                inv.append(1.0 / jnp.sum(e, axis=-1, keepdims=True))
                ps.append(e.astype(BF16))
            o = jnp.dot(jnp.concatenate(ps, axis=1), vst, preferred_element_type=F32)
            o = o * jnp.where(lane_o < HEAD_DIM, inv[0], inv[1])
            o_ref[0, pl.ds(q0, GRID_W), lanes] = o.astype(BF16)
        return carry

    lax.fori_loop(0, tq // GRID_W, row, 0)


def _nbr_attention(q, k, v, bias, tq):
    b, s, _ = q.shape
    return pl.pallas_call(
        functools.partial(_nbr_body, seq=s, tq=tq),
        grid=(b, s // tq),
        in_specs=[pl.BlockSpec((1, tq, B_W), lambda i, j: (i, j, 0)),
                  pl.BlockSpec((1, s, B_W), lambda i, j: (i, 0, 0)),
                  pl.BlockSpec((1, s, B_W), lambda i, j: (i, 0, 0)),
                  pl.BlockSpec(bias.shape, lambda i, j: (0, 0, 0, 0))],
        out_specs=pl.BlockSpec((1, tq, B_W), lambda i, j: (i, j, 0)),
        out_shape=jax.ShapeDtypeStruct((b, s, B_W), BF16),
        compiler_params=_cparams("arbitrary", "arbitrary"),
        name="nbr_attention",
    )(q, k, v, bias)


def _sigmoid(z):
    return 1.0 / (1.0 + jnp.exp(-z))


def _mix_body(x_ref, aa_ref, ab_ref, ga_ref, gb_ref, mod_ref, wa_ref, wb_ref, wo_ref,
              gpost_ref, gpre_ref, wr_ref, br_ref, cnt_in_ref,
              x1_ref, h2_ref, route_ref, cnt_ref, run_ref, *, tm):
    first = (pl.program_id(0) == 0) & (pl.program_id(1) == 0)

    @pl.when(first)
    def _():
        run_ref[...] = cnt_in_ref[...]

    ya = jnp.dot(aa_ref[0], wa_ref[...], preferred_element_type=F32)
    yb = jnp.dot(ab_ref[0], wb_ref[...], preferred_element_type=F32)
    merged = _sigmoid(ga_ref[0].astype(F32)) * ya + _sigmoid(gb_ref[0].astype(F32)) * yb
    z = jnp.dot(merged.astype(BF16), wo_ref[...], preferred_element_type=F32)
    gt1 = mod_ref[0, 2:3, :]
    sh2 = mod_ref[0, 3:4, :]
    sc2 = mod_ref[0, 4:5, :]
    x1 = x_ref[0] + gt1 * _rms(z, gpost_ref[...])
    x1_ref[0] = x1
    h2 = _rms(x1, gpre_ref[...]) * (1.0 + sc2) + sh2
    h2_ref[0] = h2

    logits = jnp.dot(h2.astype(BF16), wr_ref[...], preferred_element_type=F32) + br_ref[...]
    lane = lax.broadcasted_iota(jnp.int32, (tm, LANES), 1).astype(F32)
    work = logits
    sel = jnp.zeros((tm, LANES), F32)
    vals, idxs = [], []
    for _ in range(TOP_K):
        m = jnp.max(work, axis=-1, keepdims=True)
        idx = jnp.min(jnp.where(work == m, lane, float(LANES)), axis=-1, keepdims=True)
        hit = lane == idx
        vals.append(m)
        idxs.append(idx)
        work = jnp.where(hit, -jnp.inf, work)
        sel = sel + hit.astype(F32)
    es = [jnp.exp(v - vals[0]) for v in vals]
    den = es[0] + es[1] + es[2] + es[3]

    ri = lax.broadcasted_iota(jnp.int32, (tm, tm), 0)
    ci = lax.broadcasted_iota(jnp.int32, (tm, tm), 1)
    tri = (ci < ri).astype(BF16)
    before = jnp.dot(tri, sel.astype(BF16), preferred_element_type=F32) + run_ref[...]
    out = jnp.zeros((tm, LANES), F32)
    for k in range(TOP_K):
        rank = jnp.sum(jnp.where(lane == idxs[k], before, 0.0), axis=-1, keepdims=True)
        out = jnp.where(lane == k, idxs[k].astype(F32), out)
        out = jnp.where(lane == TOP_K + k, rank, out)
        out = jnp.where(lane == 2 * TOP_K + k, es[k] / den, out)
    route_ref[0] = out
    run_ref[...] = run_ref[...] + jnp.sum(sel, axis=0, keepdims=True)
    cnt_ref[...] = run_ref[...]


def _mix_and_route(x, aa, ab, ga, gb, mod, wa, wb, wo, gpost, gpre, wr, br, cnt_in, tm):
    b, s, d = x.shape
    tok = lambda w: pl.BlockSpec((1, tm, w), lambda i, j: (i, j, 0))
    full = lambda a: pl.BlockSpec(a.shape, lambda i, j: (0,) * a.ndim)
    return pl.pallas_call(
        functools.partial(_mix_body, tm=tm),
        grid=(b, s // tm),
        in_specs=[tok(d), tok(A_Q), tok(B_W), tok(d), tok(d),
                  pl.BlockSpec((1, 6, d), lambda i, j: (i, 0, 0)),
                  full(wa), full(wb), full(wo), full(gpost), full(gpre), full(wr), full(br),
                  full(cnt_in)],
        out_specs=[tok(d), tok(d), tok(LANES), pl.BlockSpec((1, LANES), lambda i, j: (0, 0))],
        out_shape=[jax.ShapeDtypeStruct((b, s, d), F32), jax.ShapeDtypeStruct((b, s, d), F32),
                   jax.ShapeDtypeStruct((b, s, LANES), F32), jax.ShapeDtypeStruct((1, LANES), F32)],
        scratch_shapes=[pltpu.VMEM((1, LANES), F32)],
        compiler_params=_cparams("arbitrary", "arbitrary"),
        name="mix_and_route",
    )(x, aa, ab, ga, gb, mod, wa, wb, wo, gpost, gpre, wr, br, cnt_in)


def _dispatch_body(idx_ref, hp_ref, hs_ref, z_ref, o_ref, sem, *, np_steps, ns_steps):
    i = pl.program_id(0)

    def scatter(src_ref):
        def issue(r, c):
            for k in range(TOP_K):
                pltpu.make_async_copy(src_ref.at[pl.ds(r, 1)],
                                      o_ref.at[pl.ds(idx_ref[r * TOP_K + k], 1)], sem).start()
            return c

        lax.fori_loop(0, ROW_BLOCK, issue, 0)

        def drain(r, c):
            for k in range(TOP_K):
                pltpu.make_async_copy(src_ref.at[pl.ds(0, 1)], o_ref.at[pl.ds(0, 1)], sem).wait()
            return c

        lax.fori_loop(0, ROW_BLOCK, drain, 0)

    @pl.when(i < np_steps)
    def _():
        scatter(hp_ref)

    @pl.when((i >= np_steps) & (i < np_steps + ns_steps))
    def _():
        scatter(hs_ref)

    @pl.when(i >= np_steps + ns_steps)
    def _():
        scatter(z_ref)


def _dispatch(dest, h2p, h2s, n_sorted):
    d = h2p.shape[1]
    np_steps = h2p.shape[0] // ROW_BLOCK
    ns_steps = h2s.shape[0] // ROW_BLOCK
    n_steps = dest.shape[0] // (ROW_BLOCK * TOP_K)
    zeros = jnp.zeros((ROW_BLOCK, d), F32)
    return pl.pallas_call(
        functools.partial(_dispatch_body, np_steps=np_steps, ns_steps=ns_steps),
        grid=(n_steps,),
        in_specs=[pl.BlockSpec((ROW_BLOCK * TOP_K,), lambda i: (i,), memory_space=pltpu.SMEM),
                  pl.BlockSpec((ROW_BLOCK, d), lambda i: (jnp.minimum(i, np_steps - 1), 0)),
                  pl.BlockSpec((ROW_BLOCK, d),
                               lambda i: (jnp.clip(i - np_steps, 0, ns_steps - 1), 0)),
                  pl.BlockSpec((ROW_BLOCK, d), lambda i: (0, 0))],
        out_specs=pl.BlockSpec(memory_space=pl.ANY),
        out_shape=jax.ShapeDtypeStruct((n_sorted, d), F32),
        scratch_shapes=[pltpu.SemaphoreType.DMA],
        compiler_params=_cparams("arbitrary"),
        name="moe_dispatch",
    )(dest, h2p, h2s, zeros)


def _gmm_body(te_ref, nu_ref, x_ref, wg_ref, bg_ref, wu_ref, bu_ref, wd_ref, bd_ref, y_ref):
    i = pl.program_id(0)

    @pl.when(i < nu_ref[0])
    def _():
        x = x_ref[...].astype(BF16)
        g = jnp.minimum(jnp.dot(x, wg_ref[0], preferred_element_type=F32) + bg_ref[0], SWIGLU_LIMIT)
        u = jnp.clip(jnp.dot(x, wu_ref[0], preferred_element_type=F32) + bu_ref[0],
                     -SWIGLU_LIMIT, SWIGLU_LIMIT)
        act = (u + 1.0) * (g * _sigmoid(SWIGLU_ALPHA * g))
        y_ref[...] = jnp.dot(act.astype(BF16), wd_ref[0], preferred_element_type=F32) + bd_ref[0]

    @pl.when(i >= nu_ref[0])
    def _():
        y_ref[...] = jnp.zeros_like(y_ref)


def _grouped_ffn(tile_expert, n_used, xs, wg, bg, wu, bu, wd, bd):
    n, d = xs.shape
    f = wg.shape[2]
    wspec = lambda a, b_: pl.BlockSpec((1, a, b_), lambda i, te, nu: (te[i], 0, 0))
    return pl.pallas_call(
        _gmm_body,
        grid_spec=pltpu.PrefetchScalarGridSpec(
            num_scalar_prefetch=2,
            grid=(n // MOE_TILE,),
            in_specs=[pl.BlockSpec((MOE_TILE, d), lambda i, te, nu: (i, 0)),
                      wspec(d, f), wspec(1, f), wspec(d, f), wspec(1, f), wspec(f, d), wspec(1, d)],
            out_specs=pl.BlockSpec((MOE_TILE, d), lambda i, te, nu: (i, 0)),
        ),
        out_shape=jax.ShapeDtypeStruct((n, d), F32),
        compiler_params=_cparams("arbitrary"),
        name="moe_grouped_ffn",
    )(tile_expert, n_used, xs, wg, bg, wu, bu, wd, bd)


def _combine_body(pos_ref, ys_ref, route_ref, x1_ref, mod_ref, g_ref, o_ref, buf, sem, *, tm):
    def issue(r, c):
        for k in range(TOP_K):
            pltpu.make_async_copy(ys_ref.at[pl.ds(pos_ref[r * TOP_K + k], 1)],
                                  buf.at[k, pl.ds(r, 1)], sem).start()
        return c

    lax.fori_loop(0, tm, issue, 0)

    def drain(r, c):
        for k in range(TOP_K):
            pltpu.make_async_copy(ys_ref.at[pl.ds(0, 1)], buf.at[0, pl.ds(0, 1)], sem).wait()
        return c

    lax.fori_loop(0, tm, drain, 0)

    route = route_ref[0]
    y2 = route[:, 2 * TOP_K:2 * TOP_K + 1] * buf[0]
    for k in range(1, TOP_K):
        y2 = y2 + route[:, 2 * TOP_K + k:2 * TOP_K + k + 1] * buf[k]
    gt2 = mod_ref[0, 5:6, :]
    o_ref[0] = x1_ref[0] + gt2 * _rms(y2, g_ref[...])


def _combine(pos, ys, route, x1, mod, g_post, tm):
    b, s, d = x1.shape
    spb = s // tm
    return pl.pallas_call(
        functools.partial(_combine_body, tm=tm),
        grid=(b, spb),
        in_specs=[pl.BlockSpec((tm * TOP_K,), lambda i, j: (i * spb + j,), memory_space=pltpu.SMEM),
                  pl.BlockSpec(memory_space=pl.ANY),
                  pl.BlockSpec((1, tm, LANES), lambda i, j: (i, j, 0)),
                  pl.BlockSpec((1, tm, d), lambda i, j: (i, j, 0)),
                  pl.BlockSpec((1, 6, d), lambda i, j: (i, 0, 0)),
                  pl.BlockSpec((1, d), lambda i, j: (0, 0))],
        out_specs=pl.BlockSpec((1, tm, d), lambda i, j: (i, j, 0)),
        out_shape=jax.ShapeDtypeStruct((b, s, d), F32),
        scratch_shapes=[pltpu.VMEM((TOP_K, tm, d), F32), pltpu.SemaphoreType.DMA],
        compiler_params=_cparams("arbitrary", "arbitrary"),
        name="moe_combine",
    )(pos, ys, route, x1, mod, g_post.reshape(1, d))


def _rope_tables(s_max):
    half = ROT_DIM // 2
    inv = jnp.power(jnp.float32(ROPE_THETA), -jnp.arange(0, ROT_DIM, 2, dtype=F32) / ROT_DIM)
    ang = jnp.arange(s_max, dtype=F32)[:, None] * inv[None, :]
    cos, sin = jnp.cos(ang), jnp.sin(ang)
    d = np.arange(LANES) % HEAD_DIM
    lo = jnp.asarray(d < half)
    hi = jnp.asarray((d >= half) & (d < ROT_DIM))
    cos_l = cos[:, d % half]
    sin_l = sin[:, d % half]
    c = jnp.where(lo | hi, cos_l, 1.0)
    s1 = jnp.where(lo, -sin_l, 0.0)
    s2 = jnp.where(hi, sin_l, 0.0)
    return jnp.stack([c, s1, s2]).astype(F32)


def _route_plan(route_p, route_s, counts):
    t_p = route_p.shape[0]
    route = jnp.concatenate([route_p[:, :2 * TOP_K], route_s[:, :2 * TOP_K]], axis=0)
    idx = route[:, :TOP_K].astype(jnp.int32)
    rank = route[:, TOP_K:2 * TOP_K].astype(jnp.int32)
    cnt = counts[0, :N_EXPERTS].astype(jnp.int32)
    tiles = (cnt + MOE_TILE - 1) // MOE_TILE
    tile_end = jnp.cumsum(tiles)
    start = (tile_end - tiles) * MOE_TILE
    experts = jnp.arange(N_EXPERTS, dtype=jnp.int32)

    def lookup(table, e):
        return jnp.sum(jnp.where(e[..., None] == experts, table, 0), axis=-1)

    pos = (lookup(start, idx) + rank).reshape(-1)
    n_assign = pos.shape[0]
    n_tiles = n_assign // MOE_TILE + N_EXPERTS
    n_sorted = n_tiles * MOE_TILE
    tile_id = jnp.arange(n_tiles, dtype=jnp.int32)
    tile_expert = jnp.minimum(jnp.sum(tile_end[None, :] <= tile_id[:, None], axis=-1),
                              N_EXPERTS - 1).astype(jnp.int32)
    n_used = tile_end[-1:].astype(jnp.int32)
    n_pad = n_sorted - n_assign
    pad_cnt = tiles * MOE_TILE - cnt
    pad_end = jnp.cumsum(pad_cnt)
    j = jnp.arange(n_pad, dtype=jnp.int32)
    e = jnp.sum(pad_end[None, :] <= j[:, None], axis=-1).astype(jnp.int32)
    in_group = lookup(start + cnt, e) + (j - lookup(pad_end - pad_cnt, e))
    tail = tile_end[-1] * MOE_TILE + (j - pad_end[-1])
    pad_rows = jnp.where(e < N_EXPERTS, in_group, tail).astype(jnp.int32)
    dest = jnp.concatenate([pos, pad_rows])
    return pos[:t_p * TOP_K], pos[t_p * TOP_K:], dest, tile_expert, n_used, n_sorted


def kernel(x_prompt, x_sample, c_prompt, c_sample, ada_w, ada_b, g_pre_mix, g_post_mix, g_pre_ffn,
           g_post_ffn, w_in, sink_a, rpb_b, w_branch_a, w_branch_b, w_out, w_router, b_router,
           w_gate, b_gate, w_up, b_up, w_down, b_down):
    d = D_MODEL
    scale = HEAD_DIM ** -0.5
    g = A_HEADS // A_KV_HEADS
    w = w_in[0]
    col_scale = np.ones((w.shape[1],), np.float32)
    col_scale[:A_Q] = scale
    qb0 = A_Q + 2 * A_KV
    col_scale[qb0:qb0 + B_W] = scale
    wq = w[:, :A_Q].reshape(d, A_KV_HEADS, g, HEAD_DIM).transpose(0, 2, 1, 3).reshape(d, A_Q)
    w_in_bf = (jnp.concatenate([wq, w[:, A_Q:]], axis=1) * col_scale).astype(BF16)
    wa_bf = w_branch_a[0].reshape(A_KV_HEADS, g, HEAD_DIM, d).transpose(1, 0, 2, 3).reshape(
        A_Q, d).astype(BF16)
    wb_bf = w_branch_b[0].astype(BF16)
    wo_bf = w_out[0].astype(BF16)
    wr_bf = jnp.zeros((d, LANES), F32).at[:, :N_EXPERTS].set(w_router[0]).astype(BF16)
    br = jnp.full((1, LANES), NEG, F32).at[0, :N_EXPERTS].set(b_router[0])
    wg_bf, wu_bf, wd_bf = w_gate[0].astype(BF16), w_up[0].astype(BF16), w_down[0].astype(BF16)
    bg, bu, bd = (t[0].reshape(N_EXPERTS, 1, -1) for t in (b_gate, b_up, b_down))
    sink_rows = jnp.repeat(sink_a[0].reshape(A_KV_HEADS, g).T, A_BLOCK, axis=0).astype(F32)

    nb_p = x_prompt.shape[0]
    mod = _modulation(jnp.concatenate([c_prompt, c_sample], axis=0), ada_w[0], ada_b[0])
    mod = mod.reshape(-1, 6, d)
    s_max = max(x_prompt.shape[1], x_sample.shape[1])
    rope = _rope_tables(s_max)
    bias = _nbr_bias(rpb_b[0])

    def front(x, m, cnt_in):
        qa, ka, va, qb, kb, vb, ga, gb = _in_projection(x, m, g_pre_mix[0], w_in_bf, rope, 512)
        aa = _window_attention(qa, ka, va, sink_rows, 512)
        ab = _nbr_attention(qb, kb, vb, bias, 512)
        return _mix_and_route(x, aa, ab, ga, gb, m, wa_bf, wb_bf, wo_bf,
                              g_post_mix[0].reshape(1, d), g_pre_ffn[0].reshape(1, d),
                              wr_bf, br, cnt_in, 512)

    mod_p, mod_s = mod[:nb_p], mod[nb_p:]
    x1p, h2p, route_p, cnt_p = front(x_prompt, mod_p, jnp.zeros((1, LANES), F32))
    x1s, h2s, route_s, cnt = front(x_sample, mod_s, cnt_p)

    rp, rs = route_p.reshape(-1, LANES), route_s.reshape(-1, LANES)
    pos_p, pos_s, dest, tile_expert, n_used, n_sorted = _route_plan(rp, rs, cnt)
    xs = _dispatch(dest, h2p.reshape(-1, d), h2s.reshape(-1, d), n_sorted)
    ys = _grouped_ffn(tile_expert, n_used, xs, wg_bf, bg, wu_bf, bu, wd_bf, bd)
    y_p = _combine(pos_p, ys, route_p, x1p, mod_p, g_post_ffn[0], 256)
    y_s = _combine(pos_s, ys, route_s, x1s, mod_s, g_post_ffn[0], 256)
    return (y_p, y_s)
```

```python
import functools

import jax
import jax.numpy as jnp
import numpy as np
from jax import lax
from jax.experimental import pallas as pl
from jax.experimental.pallas import tpu as pltpu

D_MODEL = 1024
HEAD_DIM = 64
ROT_DIM = HEAD_DIM // 4
ROPE_THETA = 500000.0
A_HEADS = 8
A_KV_HEADS = 2
WINDOW = 128
A_BLOCK = 128
B_HEADS = 8
GRID_W = 64
NA_ROWS = 8
NA_COLS = 16
N_EXPERTS = 32
TOP_K = 4
SWIGLU_LIMIT = 7.0
SWIGLU_ALPHA = 1.702
RMS_EPS = 1e-6
NEG = -1e30
A_Q = A_HEADS * HEAD_DIM
A_KV = A_KV_HEADS * HEAD_DIM
B_W = B_HEADS * HEAD_DIM

LANES = 128
PAIR = 2 * HEAD_DIM
N_PAIRS = B_W // PAIR
MOE_TILE = 512
TOKEN_BLOCK = 256
ROW_CHUNK_LOG2 = 3
ROW_CHUNK = 1 << ROW_CHUNK_LOG2
LOCAL_ROWS = TOKEN_BLOCK * TOP_K + N_EXPERTS * ROW_CHUNK
VMEM_LIMIT = 56 * 1024 * 1024

BF16 = jnp.bfloat16
F32 = jnp.float32


def _cparams(*sem):
    return pltpu.CompilerParams(dimension_semantics=sem, vmem_limit_bytes=VMEM_LIMIT)


def _mod_body(c_ref, w_ref, b_ref, o_ref):
    c = c_ref[...]
    s = c * (1.0 / (1.0 + jnp.exp(-c)))
    o_ref[...] = jnp.dot(s, w_ref[...], preferred_element_type=F32,
                         precision=lax.Precision.HIGHEST) + b_ref[...]


def _modulation(c, ada_w, ada_b):
    n, d = c.shape
    blk = 1024
    return pl.pallas_call(
        _mod_body,
        grid=(ada_w.shape[1] // blk,),
        in_specs=[pl.BlockSpec((n, d), lambda j: (0, 0)),
                  pl.BlockSpec((d, blk), lambda j: (0, j)),
                  pl.BlockSpec((1, blk), lambda j: (0, j))],
        out_specs=pl.BlockSpec((n, blk), lambda j: (0, j)),
        out_shape=jax.ShapeDtypeStruct((n, ada_w.shape[1]), F32),
        compiler_params=_cparams("arbitrary"),
        name="adaln_mod",
    )(c, ada_w, ada_b.reshape(1, -1))


def _rms(x, g):
    return x * lax.rsqrt(jnp.mean(x * x, axis=-1, keepdims=True) + RMS_EPS) * g


def _rope_tile(seg, rope_ref):
    return (seg * rope_ref[0] + pltpu.roll(seg, LANES - ROT_DIM // 2, 1) * rope_ref[1]
            + pltpu.roll(seg, ROT_DIM // 2, 1) * rope_ref[2])


def _inproj_body(x_ref, mod_ref, g_ref, w_ref, rope_ref,
                 qa_ref, ka_ref, va_ref, qb_ref, kb_ref, vb_ref, ga_ref, gb_ref):
    sh1 = mod_ref[0, 0:1, :]
    sc1 = mod_ref[0, 1:2, :]
    h = (_rms(x_ref[0], g_ref[...]) * (1.0 + sc1) + sh1).astype(BF16)

    def proj(lo, hi):
        return jnp.dot(h, w_ref[:, lo:hi], preferred_element_type=F32)

    o = 0
    for t in range(A_Q // LANES):
        qa_ref[0, :, t * LANES:(t + 1) * LANES] = _rope_tile(
            proj(o + t * LANES, o + (t + 1) * LANES), rope_ref).astype(BF16)
    o += A_Q
    ka_ref[0] = _rope_tile(proj(o, o + A_KV), rope_ref).astype(BF16)
    o += A_KV
    va_ref[0] = proj(o, o + A_KV).astype(BF16)
    o += A_KV
    for ref in (qb_ref, kb_ref, vb_ref):
        ref[0] = proj(o, o + B_W).astype(BF16)
        o += B_W
    for ref in (ga_ref, gb_ref):
        ref[0] = proj(o, o + D_MODEL).astype(BF16)
        o += D_MODEL


def _in_projection(x, mod, g_pre, w_in_bf, rope, tm):
    b, s, d = x.shape
    widths = (A_Q, A_KV, A_KV, B_W, B_W, B_W, D_MODEL, D_MODEL)
    tok = lambda w: pl.BlockSpec((1, tm, w), lambda i, j: (i, j, 0))
    return pl.pallas_call(
        _inproj_body,
        grid=(b, s // tm),
        in_specs=[tok(d),
                  pl.BlockSpec((1, 6, d), lambda i, j: (i, 0, 0)),
                  pl.BlockSpec((1, d), lambda i, j: (0, 0)),
                  pl.BlockSpec(w_in_bf.shape, lambda i, j: (0, 0)),
                  pl.BlockSpec((3, tm, LANES), lambda i, j: (0, j, 0))],
        out_specs=[tok(w) for w in widths],
        out_shape=[jax.ShapeDtypeStruct((b, s, w), BF16) for w in widths],
        compiler_params=_cparams("arbitrary", "arbitrary"),
        name="in_projection",
    )(x, mod, g_pre.reshape(1, d), w_in_bf, rope)


def _stack_heads(t):
    lane = lax.broadcasted_iota(jnp.int32, t.shape, 1)
    zero = jnp.zeros_like(t)
    return jnp.concatenate([jnp.where(lane < HEAD_DIM, t, zero),
                            jnp.where(lane >= HEAD_DIM, t, zero)], axis=0)


def _window_body(q_ref, k_ref, v_ref, sink_ref, o_ref, *, seq, tq):
    span = A_BLOCK + 2 * WINDOW
    n_sub = tq // A_BLOCK
    rows = N_PAIRS * A_BLOCK
    qi = lax.broadcasted_iota(jnp.int32, (rows, 2 * span), 0) % A_BLOCK
    kj = lax.broadcasted_iota(jnp.int32, (rows, 2 * span), 1) % span
    lane_o = lax.broadcasted_iota(jnp.int32, (rows, LANES), 1)
    sink = sink_ref[...]
    for sub in range(n_sub):
        q0 = pl.program_id(1) * tq + sub * A_BLOCK
        k0 = pl.multiple_of(jnp.clip(q0 - WINDOW, 0, seq - span), A_BLOCK)
        kst = _stack_heads(k_ref[0, pl.ds(k0, span), :])
        vst = _stack_heads(v_ref[0, pl.ds(k0, span), :])
        qblk = q_ref[0, sub * A_BLOCK:(sub + 1) * A_BLOCK, :]
        qs = jnp.concatenate([qblk[:, p * LANES:(p + 1) * LANES] for p in range(N_PAIRS)], axis=0)
        s = lax.dot_general(qs, kst, (((1,), (1,)), ((), ())), preferred_element_type=F32)
        band = jnp.abs((q0 + qi) - (k0 + kj)) <= WINDOW
        s = jnp.where(band, s, NEG)
        outs = []
        ps = []
        for half in range(A_KV_HEADS):
            sh = s[:, half * span:(half + 1) * span]
            snk = sink[:, half:half + 1]
            m = jnp.maximum(jnp.max(sh, axis=-1, keepdims=True), snk)
            e = jnp.exp(sh - m)
            den = jnp.sum(e, axis=-1, keepdims=True) + jnp.exp(snk - m)
            ps.append(e.astype(BF16))
            outs.append(1.0 / den)
        p = jnp.concatenate(ps, axis=1)
        o = jnp.dot(p, vst, preferred_element_type=F32)
        o = o * jnp.where(lane_o < HEAD_DIM, outs[0], outs[1])
        for pr in range(N_PAIRS):
            o_ref[0, sub * A_BLOCK:(sub + 1) * A_BLOCK, pr * LANES:(pr + 1) * LANES] = (
                o[pr * A_BLOCK:(pr + 1) * A_BLOCK, :].astype(BF16))


def _window_attention(q, k, v, sink_rows, tq):
    b, s, _ = q.shape
    return pl.pallas_call(
        functools.partial(_window_body, seq=s, tq=tq),
        grid=(b, s // tq),
        in_specs=[pl.BlockSpec((1, tq, A_Q), lambda i, j: (i, j, 0)),
                  pl.BlockSpec((1, s, A_KV), lambda i, j: (i, 0, 0)),
                  pl.BlockSpec((1, s, A_KV), lambda i, j: (i, 0, 0)),
                  pl.BlockSpec(sink_rows.shape, lambda i, j: (0, 0))],
        out_specs=pl.BlockSpec((1, tq, A_Q), lambda i, j: (i, j, 0)),
        out_shape=jax.ShapeDtypeStruct((b, s, A_Q), BF16),
        compiler_params=_cparams("arbitrary", "arbitrary"),
        name="window_attention",
    )(q, k, v, sink_rows)


def _nbr_bias_body(rpb_ref, o_ref):
    p = pl.program_id(0)
    d = pl.program_id(1)
    nkeys = NA_ROWS * GRID_W
    qc = lax.broadcasted_iota(jnp.int32, (GRID_W, 2 * nkeys), 0)
    col = lax.broadcasted_iota(jnp.int32, (GRID_W, 2 * nkeys), 1)
    kc = col % GRID_W
    qstart = jnp.clip(qc - NA_COLS // 2, 0, GRID_W - NA_COLS)
    valid = (kc >= qstart) & (kc < qstart + NA_COLS)
    cidx = kc - qc + NA_COLS - 1
    col1 = lax.broadcasted_iota(jnp.int32, (1, 2 * nkeys), 1)
    blk1 = col1 // GRID_W
    acc = jnp.full((GRID_W, 2 * nkeys), NEG, F32)
    for j in range(2 * NA_COLS - 1):
        rv = jnp.zeros((1, 2 * nkeys), F32)
        for hh in range(2):
            for aa in range(NA_ROWS):
                rv = jnp.where(blk1 == hh * NA_ROWS + aa,
                               rpb_ref[2 * p + hh, aa - d + NA_ROWS - 1, j], rv)
        acc = jnp.where(valid & (cidx == j), rv, acc)
    o_ref[0, 0] = acc


def _nbr_bias(rpb):
    nkeys = NA_ROWS * GRID_W
    return pl.pallas_call(
        _nbr_bias_body,
        grid=(N_PAIRS, NA_ROWS),
        in_specs=[pl.BlockSpec(memory_space=pltpu.SMEM)],
        out_specs=pl.BlockSpec((1, 1, GRID_W, 2 * nkeys), lambda p, d: (p, d, 0, 0)),
        out_shape=jax.ShapeDtypeStruct((N_PAIRS, NA_ROWS, GRID_W, 2 * nkeys), F32),
        compiler_params=_cparams("arbitrary", "arbitrary"),
        name="nbr_bias",
    )(rpb)


def _nbr_body(q_ref, k_ref, v_ref, bias_ref, o_ref, *, seq, tq):
    n_rows = seq // GRID_W
    nkeys = NA_ROWS * GRID_W
    lane_o = lax.broadcasted_iota(jnp.int32, (GRID_W, LANES), 1)

    def row(rl, carry):
        r = pl.program_id(1) * (tq // GRID_W) + rl
        r0 = jnp.clip(r - NA_ROWS // 2, 0, n_rows - NA_ROWS)
        d = r - r0
        k0 = pl.multiple_of(r0 * GRID_W, GRID_W)
        q0 = pl.multiple_of(rl * GRID_W, GRID_W)
        for pr in range(N_PAIRS):
            lanes = slice(pr * LANES, (pr + 1) * LANES)
            kst = _stack_heads(k_ref[0, pl.ds(k0, nkeys), lanes])
            vst = _stack_heads(v_ref[0, pl.ds(k0, nkeys), lanes])
            qp = q_ref[0, pl.ds(q0, GRID_W), lanes]
            s = lax.dot_general(qp, kst, (((1,), (1,)), ((), ())), preferred_element_type=F32)
            s = s + bias_ref[pr, d]
            ps = []
            inv = []
            for half in range(2):
                sh = s[:, half * nkeys:(half + 1) * nkeys]
                m = jnp.max(sh, axis=-1, keepdims=True)
                e = jnp.exp(sh - m)
                inv.append(1.0 / jnp.sum(e, axis=-1, keepdims=True))
                ps.append(e.astype(BF16))
            o = jnp.dot(jnp.concatenate(ps, axis=1), vst, preferred_element_type=F32)
            o = o * jnp.where(lane_o < HEAD_DIM, inv[0], inv[1])
            o_ref[0, pl.ds(q0, GRID_W), lanes] = o.astype(BF16)
        return carry

    lax.fori_loop(0, tq // GRID_W, row, 0)


def _nbr_attention(q, k, v, bias, tq):
    b, s, _ = q.shape
    return pl.pallas_call(
        functools.partial(_nbr_body, seq=s, tq=tq),
        grid=(b, s // tq),
        in_specs=[pl.BlockSpec((1, tq, B_W), lambda i, j: (i, j, 0)),
                  pl.BlockSpec((1, s, B_W), lambda i, j: (i, 0, 0)),
                  pl.BlockSpec((1, s, B_W), lambda i, j: (i, 0, 0)),
                  pl.BlockSpec(bias.shape, lambda i, j: (0, 0, 0, 0))],
        out_specs=pl.BlockSpec((1, tq, B_W), lambda i, j: (i, j, 0)),
        out_shape=jax.ShapeDtypeStruct((b, s, B_W), BF16),
        compiler_params=_cparams("arbitrary", "arbitrary"),
        name="nbr_attention",
    )(q, k, v, bias)


def _sigmoid(z):
    return 1.0 / (1.0 + jnp.exp(-z))


def _mix_body(x_ref, aa_ref, ab_ref, ga_ref, gb_ref, mod_ref, wa_ref, wb_ref, wo_ref,
              gpost_ref, gpre_ref, wr_ref, br_ref, cnt_in_ref,
              x1_ref, h2_ref, route_ref, blk_ref, cnt_ref, run_ref, *, tm):
    first = (pl.program_id(0) == 0) & (pl.program_id(1) == 0)

    @pl.when(first)
    def _():
        run_ref[...] = cnt_in_ref[...]

    ya = jnp.dot(aa_ref[0], wa_ref[...], preferred_element_type=F32)
    yb = jnp.dot(ab_ref[0], wb_ref[...], preferred_element_type=F32)
    merged = _sigmoid(ga_ref[0].astype(F32)) * ya + _sigmoid(gb_ref[0].astype(F32)) * yb
    z = jnp.dot(merged.astype(BF16), wo_ref[...], preferred_element_type=F32)
    gt1 = mod_ref[0, 2:3, :]
    sh2 = mod_ref[0, 3:4, :]
    sc2 = mod_ref[0, 4:5, :]
    x1 = x_ref[0] + gt1 * _rms(z, gpost_ref[...])
    x1_ref[0] = x1
    h2 = _rms(x1, gpre_ref[...]) * (1.0 + sc2) + sh2
    h2_bf = h2.astype(BF16)
    h2_ref[0] = h2_bf

    logits = jnp.dot(h2_bf, wr_ref[...], preferred_element_type=F32) + br_ref[...]
    lane = lax.broadcasted_iota(jnp.int32, (tm, LANES), 1).astype(F32)
    work = logits
    sel = jnp.zeros((tm, LANES), F32)
    vals, idxs = [], []
    for _ in range(TOP_K):
        m = jnp.max(work, axis=-1, keepdims=True)
        idx = jnp.min(jnp.where(work == m, lane, float(LANES)), axis=-1, keepdims=True)
        hit = lane == idx
        vals.append(m)
        idxs.append(idx)
        work = jnp.where(hit, -jnp.inf, work)
        sel = sel + hit.astype(F32)
    es = [jnp.exp(v - vals[0]) for v in vals]
    den = es[0] + es[1] + es[2] + es[3]

    ri = lax.broadcasted_iota(jnp.int32, (tm, tm), 0)
    ci = lax.broadcasted_iota(jnp.int32, (tm, tm), 1)
    tri = (ci < ri).astype(BF16)
    earlier = jnp.dot(tri, sel.astype(BF16), preferred_element_type=F32)
    n_blk = jnp.sum(sel, axis=0, keepdims=True)
    chunks = jnp.floor((n_blk + (ROW_CHUNK - 1.0)) * (1.0 / ROW_CHUNK))
    li = lax.broadcasted_iota(jnp.int32, (LANES, LANES), 0)
    lj = lax.broadcasted_iota(jnp.int32, (LANES, LANES), 1)
    upper = (li < lj).astype(BF16)
    seg_start = ROW_CHUNK * jnp.dot(jnp.broadcast_to(chunks, (8, LANES)).astype(BF16), upper,
                                    preferred_element_type=F32)[0:1, :]
    slot = earlier + seg_start
    out = jnp.zeros((tm, LANES), F32)
    for k in range(TOP_K):
        local = jnp.sum(jnp.where(lane == idxs[k], slot, 0.0), axis=-1, keepdims=True)
        out = jnp.where(lane == k, es[k] / den, out)
        out = jnp.where(lane == TOP_K + k, local, out)
    route_ref[0] = out
    row = lax.broadcasted_iota(jnp.int32, (8, LANES), 0)
    blk_ref[0] = jnp.where(row == 0, n_blk, jnp.where(row == 1, run_ref[...],
                                                     jnp.where(row == 2, seg_start, 0.0)))
    run_ref[...] = run_ref[...] + ROW_CHUNK * chunks
    cnt_ref[...] = run_ref[...]


def _mix_and_route(x, aa, ab, ga, gb, mod, wa, wb, wo, gpost, gpre, wr, br, cnt_in, tm):
    b, s, d = x.shape
    tok = lambda w: pl.BlockSpec((1, tm, w), lambda i, j: (i, j, 0))
    full = lambda a: pl.BlockSpec(a.shape, lambda i, j: (0,) * a.ndim)
    return pl.pallas_call(
        functools.partial(_mix_body, tm=tm),
        grid=(b, s // tm),
        in_specs=[tok(d), tok(A_Q), tok(B_W), tok(d), tok(d),
                  pl.BlockSpec((1, 6, d), lambda i, j: (i, 0, 0)),
                  full(wa), full(wb), full(wo), full(gpost), full(gpre), full(wr), full(br),
                  full(cnt_in)],
        out_specs=[tok(d), tok(d), tok(LANES), pl.BlockSpec((1, 8, LANES), lambda i, j: (i, j, 0)),
                   pl.BlockSpec((1, LANES), lambda i, j: (0, 0))],
        out_shape=[jax.ShapeDtypeStruct((b, s, d), F32), jax.ShapeDtypeStruct((b, s, d), BF16),
                   jax.ShapeDtypeStruct((b, s, LANES), F32),
                   jax.ShapeDtypeStruct((b, (s // tm) * 8, LANES), F32),
                   jax.ShapeDtypeStruct((1, LANES), F32)],
        scratch_shapes=[pltpu.VMEM((1, LANES), F32)],
        compiler_params=_cparams("arbitrary", "arbitrary"),
        name="mix_and_route",
    )(x, aa, ab, ga, gb, mod, wa, wb, wo, gpost, gpre, wr, br, cnt_in)


def _for_each_chunk(tab_ref, fn):
    def per_expert(e, carry):
        n = tab_ref[e]
        sorted0 = tab_ref[N_EXPERTS + e]
        local0 = tab_ref[2 * N_EXPERTS + e]

        def chunk(c, carry2):
            fn(pl.multiple_of(local0 + c * ROW_CHUNK, ROW_CHUNK),
               pl.multiple_of(sorted0 + c * ROW_CHUNK, ROW_CHUNK))
            return carry2

        lax.fori_loop(0, lax.shift_right_logical(n + (ROW_CHUNK - 1), ROW_CHUNK_LOG2), chunk, 0)
        return carry

    lax.fori_loop(0, N_EXPERTS, per_expert, 0)


def _dispatch_body(tab_ref, tail_ref, rt_ref, hp_ref, hs_ref, xs_ref, ws_ref,
                   loc_ref, wloc_ref, zx_ref, zw_ref, sem_x, sem_w, *, np_steps):
    i = pl.program_id(0)

    def x_copy(src, l, s):
        return pltpu.make_async_copy(src.at[pl.ds(l, ROW_CHUNK)], xs_ref.at[pl.ds(s, ROW_CHUNK)], sem_x)

    def w_copy(src, l, s):
        return pltpu.make_async_copy(src.at[pl.ds(l, ROW_CHUNK)], ws_ref.at[pl.ds(s, ROW_CHUNK)], sem_w)

    @pl.when(i == 0)
    def _():
        zx_ref[...] = jnp.zeros_like(zx_ref)
        zw_ref[...] = jnp.zeros_like(zw_ref)

        def tails(fn):
            def per_expert(e, carry):
                def chunk(c, carry2):
                    fn(pl.multiple_of(tail_ref[e] + c * ROW_CHUNK, ROW_CHUNK))
                    return carry2

                lax.fori_loop(0, tail_ref[N_EXPERTS + e], chunk, 0)
                return carry

            lax.fori_loop(0, N_EXPERTS, per_expert, 0)

        def start(s):
            x_copy(zx_ref, 0, s).start()
            w_copy(zw_ref, 0, s).start()

        def wait(s):
            x_copy(zx_ref, 0, s).wait()
            w_copy(zw_ref, 0, s).wait()

        tails(start)
        tails(wait)

        def tile_copies(t):
            r = pl.multiple_of(t * MOE_TILE, MOE_TILE)
            return (pltpu.make_async_copy(zx_ref, xs_ref.at[pl.ds(r, MOE_TILE)], sem_x),
                    pltpu.make_async_copy(zw_ref, ws_ref.at[pl.ds(r, MOE_TILE)], sem_w))

        def unused(fn):
            def tile(t, carry):
                for cp in tile_copies(t):
                    fn(cp)
                return carry

            lax.fori_loop(tail_ref[2 * N_EXPERTS], xs_ref.shape[0] // MOE_TILE, tile, 0)

        unused(lambda cp: cp.start())
        unused(lambda cp: cp.wait())

    rt = rt_ref[0]
    srow = lax.broadcasted_iota(jnp.int32, (LOCAL_ROWS, TOKEN_BLOCK), 0).astype(F32)
    hit_any = None
    wsel = jnp.zeros((LOCAL_ROWS, TOKEN_BLOCK), F32)
    for k in range(TOP_K):
        hit = srow == rt[TOP_K + k:TOP_K + k + 1, :]
        wsel = jnp.where(hit, rt[k:k + 1, :], wsel)
        hit_any = hit if hit_any is None else (hit_any | hit)
    onehot = jnp.where(hit_any, 1.0, 0.0).astype(BF16)

    @pl.when(i < np_steps)
    def _():
        loc_ref[...] = jnp.dot(onehot, hp_ref[...], preferred_element_type=F32)

    @pl.when(i >= np_steps)
    def _():
        loc_ref[...] = jnp.dot(onehot, hs_ref[...], preferred_element_type=F32)

    wloc_ref[...] = jnp.broadcast_to(jnp.sum(wsel, axis=1, keepdims=True), (LOCAL_ROWS, LANES))

    def start(l, s):
        x_copy(loc_ref, l, s).start()
        w_copy(wloc_ref, l, s).start()

    def wait(l, s):
        x_copy(loc_ref, l, s).wait()
        w_copy(wloc_ref, l, s).wait()

    _for_each_chunk(tab_ref, start)
    _for_each_chunk(tab_ref, wait)


def _dispatch(tab, tail, rt, h2p, h2s, n_sorted):
    d = h2p.shape[1]
    np_steps = h2p.shape[0] // TOKEN_BLOCK
    ns_steps = h2s.shape[0] // TOKEN_BLOCK
    return pl.pallas_call(
        functools.partial(_dispatch_body, np_steps=np_steps),
        grid=(np_steps + ns_steps,),
        in_specs=[pl.BlockSpec((LANES,), lambda i: (i,), memory_space=pltpu.SMEM),
                  pl.BlockSpec((LANES,), lambda i: (0,), memory_space=pltpu.SMEM),
                  pl.BlockSpec((1, 8, TOKEN_BLOCK), lambda i: (i, 0, 0)),
                  pl.BlockSpec((TOKEN_BLOCK, d), lambda i: (jnp.minimum(i, np_steps - 1), 0)),
                  pl.BlockSpec((TOKEN_BLOCK, d),
                               lambda i: (jnp.clip(i - np_steps, 0, ns_steps - 1), 0))],
        out_specs=[pl.BlockSpec(memory_space=pl.ANY), pl.BlockSpec(memory_space=pl.ANY)],
        out_shape=[jax.ShapeDtypeStruct((n_sorted, d), F32),
                   jax.ShapeDtypeStruct((n_sorted, LANES), F32)],
        scratch_shapes=[pltpu.VMEM((LOCAL_ROWS, d), F32), pltpu.VMEM((LOCAL_ROWS, LANES), F32),
                        pltpu.VMEM((MOE_TILE, d), F32), pltpu.VMEM((MOE_TILE, LANES), F32),
                        pltpu.SemaphoreType.DMA, pltpu.SemaphoreType.DMA],
        compiler_params=_cparams("arbitrary"),
        name="moe_dispatch",
    )(tab, tail, rt, h2p, h2s)


def _gmm_body(te_ref, nu_ref, x_ref, ws_ref, wg_ref, bg_ref, wu_ref, bu_ref, wd_ref, bd_ref, y_ref):
    i = pl.program_id(0)

    @pl.when(i < nu_ref[0])
    def _():
        x = x_ref[...].astype(BF16)
        g = jnp.minimum(jnp.dot(x, wg_ref[0], preferred_element_type=F32) + bg_ref[0], SWIGLU_LIMIT)
        u = jnp.clip(jnp.dot(x, wu_ref[0], preferred_element_type=F32) + bu_ref[0],
                     -SWIGLU_LIMIT, SWIGLU_LIMIT)
        act = (u + 1.0) * (g * _sigmoid(SWIGLU_ALPHA * g))
        y = jnp.dot(act.astype(BF16), wd_ref[0], preferred_element_type=F32) + bd_ref[0]
        y_ref[...] = ws_ref[:, 0:1] * y

    @pl.when(i >= nu_ref[0])
    def _():
        y_ref[...] = jnp.zeros_like(y_ref)


def _grouped_ffn(tile_expert, n_used, xs, ws, wg, bg, wu, bu, wd, bd):
    n, d = xs.shape
    f = wg.shape[2]
    wspec = lambda a, b_: pl.BlockSpec((1, a, b_), lambda i, te, nu: (te[i], 0, 0))
    row = lambda w: pl.BlockSpec((MOE_TILE, w), lambda i, te, nu: (jnp.minimum(i, nu[0] - 1), 0))
    return pl.pallas_call(
        _gmm_body,
        grid_spec=pltpu.PrefetchScalarGridSpec(
            num_scalar_prefetch=2,
            grid=(n // MOE_TILE,),
            in_specs=[row(d), row(LANES),
                      wspec(d, f), wspec(1, f), wspec(d, f), wspec(1, f), wspec(f, d), wspec(1, d)],
            out_specs=pl.BlockSpec((MOE_TILE, d), lambda i, te, nu: (i, 0)),
        ),
        out_shape=jax.ShapeDtypeStruct((n, d), F32),
        compiler_params=_cparams("arbitrary"),
        name="moe_grouped_ffn",
    )(tile_expert, n_used, xs, ws, wg, bg, wu, bu, wd, bd)


def _combine_body(tab_ref, ys_ref, route_ref, x1_ref, mod_ref, g_ref, o_ref, loc_ref, sem):
    first = (pl.program_id(0) == 0) & (pl.program_id(1) == 0)

    @pl.when(first)
    def _():
        loc_ref[...] = jnp.zeros_like(loc_ref)

    def copy(l, s):
        return pltpu.make_async_copy(ys_ref.at[pl.ds(s, ROW_CHUNK)], loc_ref.at[pl.ds(l, ROW_CHUNK)], sem)

    _for_each_chunk(tab_ref, lambda l, s: copy(l, s).start())
    _for_each_chunk(tab_ref, lambda l, s: copy(l, s).wait())

    route = route_ref[0]
    col = lax.broadcasted_iota(jnp.int32, (TOKEN_BLOCK, LOCAL_ROWS), 1).astype(F32)
    hit_any = None
    for k in range(TOP_K):
        hit = col == route[:, TOP_K + k:TOP_K + k + 1]
        hit_any = hit if hit_any is None else (hit_any | hit)
    onehot = jnp.where(hit_any, 1.0, 0.0).astype(BF16)
    y2 = jnp.dot(onehot, loc_ref[...].astype(BF16), preferred_element_type=F32)
    gt2 = mod_ref[0, 5:6, :]
    o_ref[0] = x1_ref[0] + gt2 * _rms(y2, g_ref[...])


def _combine(tab, first_block, ys, route, x1, mod, g_post):
    b, s, d = x1.shape
    spb = s // TOKEN_BLOCK
    tok = lambda w: pl.BlockSpec((1, TOKEN_BLOCK, w), lambda i, j: (i, j, 0))
    return pl.pallas_call(
        _combine_body,
        grid=(b, spb),
        in_specs=[pl.BlockSpec((LANES,), lambda i, j: (first_block + i * spb + j,),
                               memory_space=pltpu.SMEM),
                  pl.BlockSpec(memory_space=pl.ANY),
                  tok(LANES), tok(d),
                  pl.BlockSpec((1, 6, d), lambda i, j: (i, 0, 0)),
                  pl.BlockSpec((1, d), lambda i, j: (0, 0))],
        out_specs=tok(d),
        out_shape=jax.ShapeDtypeStruct((b, s, d), F32),
        scratch_shapes=[pltpu.VMEM((LOCAL_ROWS, d), F32), pltpu.SemaphoreType.DMA],
        compiler_params=_cparams("arbitrary", "arbitrary"),
        name="moe_combine",
    )(tab, ys, route, x1, mod, g_post.reshape(1, d))


def _rope_tables(s_max):
    half = ROT_DIM // 2
    inv = jnp.power(jnp.float32(ROPE_THETA), -jnp.arange(0, ROT_DIM, 2, dtype=F32) / ROT_DIM)
    ang = jnp.arange(s_max, dtype=F32)[:, None] * inv[None, :]
    cos, sin = jnp.cos(ang), jnp.sin(ang)
    d = np.arange(LANES) % HEAD_DIM
    lo = jnp.asarray(d < half)
    hi = jnp.asarray((d >= half) & (d < ROT_DIM))
    cos_l = cos[:, d % half]
    sin_l = sin[:, d % half]
    c = jnp.where(lo | hi, cos_l, 1.0)
    s1 = jnp.where(lo, -sin_l, 0.0)
    s2 = jnp.where(hi, sin_l, 0.0)
    return jnp.stack([c, s1, s2]).astype(F32)


def _route_plan(blk_p, blk_s, counts, n_assign):
    blk = jnp.concatenate([blk_p.reshape(-1, 8, LANES), blk_s.reshape(-1, 8, LANES)], axis=0)
    n_blk = blk[:, 0, :N_EXPERTS].astype(jnp.int32)
    before = blk[:, 1, :N_EXPERTS].astype(jnp.int32)
    local0 = blk[:, 2, :N_EXPERTS].astype(jnp.int32)
    cnt = counts[0, :N_EXPERTS].astype(jnp.int32)
    tiles = (cnt + (MOE_TILE - 1)) // MOE_TILE
    tile_end = jnp.cumsum(tiles)
    start = (tile_end - tiles) * MOE_TILE
    tab = jnp.concatenate([n_blk, start[None, :] + before, local0, jnp.zeros_like(n_blk)],
                          axis=1).reshape(-1)
    n_pad = blk.shape[0] * N_EXPERTS * (ROW_CHUNK - 1)
    n_tiles = (n_assign + n_pad) // MOE_TILE + N_EXPERTS
    tile_id = jnp.arange(n_tiles, dtype=jnp.int32)
    tile_expert = jnp.minimum(jnp.sum(tile_end[None, :] <= tile_id[:, None], axis=-1),
                              N_EXPERTS - 1).astype(jnp.int32)
    n_used = tile_end[-1:].astype(jnp.int32)
    zero0 = start + cnt
    zero_chunks = (tile_end * MOE_TILE - zero0) // ROW_CHUNK
    tail = jnp.concatenate([zero0, zero_chunks, n_used,
                            jnp.zeros((LANES - 2 * N_EXPERTS - 1,), jnp.int32)])
    return tab, tail.astype(jnp.int32), tile_expert, n_used, n_tiles * MOE_TILE


def kernel(x_prompt, x_sample, c_prompt, c_sample, ada_w, ada_b, g_pre_mix, g_post_mix, g_pre_ffn,
           g_post_ffn, w_in, sink_a, rpb_b, w_branch_a, w_branch_b, w_out, w_router, b_router,
           w_gate, b_gate, w_up, b_up, w_down, b_down):
    d = D_MODEL
    scale = HEAD_DIM ** -0.5
    g = A_HEADS // A_KV_HEADS
    w = w_in[0]
    col_scale = np.ones((w.shape[1],), np.float32)
    col_scale[:A_Q] = scale
    qb0 = A_Q + 2 * A_KV
    col_scale[qb0:qb0 + B_W] = scale
    wq = w[:, :A_Q].reshape(d, A_KV_HEADS, g, HEAD_DIM).transpose(0, 2, 1, 3).reshape(d, A_Q)
    w_in_bf = (jnp.concatenate([wq, w[:, A_Q:]], axis=1) * col_scale).astype(BF16)
    wa_bf = w_branch_a[0].reshape(A_KV_HEADS, g, HEAD_DIM, d).transpose(1, 0, 2, 3).reshape(
        A_Q, d).astype(BF16)
    wb_bf = w_branch_b[0].astype(BF16)
    wo_bf = w_out[0].astype(BF16)
    wr_bf = jnp.zeros((d, LANES), F32).at[:, :N_EXPERTS].set(w_router[0]).astype(BF16)
    br = jnp.full((1, LANES), NEG, F32).at[0, :N_EXPERTS].set(b_router[0])
    wg_bf, wu_bf, wd_bf = w_gate[0].astype(BF16), w_up[0].astype(BF16), w_down[0].astype(BF16)
    bg, bu, bd = (t[0].reshape(N_EXPERTS, 1, -1) for t in (b_gate, b_up, b_down))
    sink_rows = jnp.repeat(sink_a[0].reshape(A_KV_HEADS, g).T, A_BLOCK, axis=0).astype(F32)

    nb_p = x_prompt.shape[0]
    mod = _modulation(jnp.concatenate([c_prompt, c_sample], axis=0), ada_w[0], ada_b[0])
    mod = mod.reshape(-1, 6, d)
    s_max = max(x_prompt.shape[1], x_sample.shape[1])
    rope = _rope_tables(s_max)
    bias = _nbr_bias(rpb_b[0])

    def front(x, m, cnt_in):
        qa, ka, va, qb, kb, vb, ga, gb = _in_projection(x, m, g_pre_mix[0], w_in_bf, rope, 512)
        aa = _window_attention(qa, ka, va, sink_rows, 512)
        ab = _nbr_attention(qb, kb, vb, bias, 512)
        return _mix_and_route(x, aa, ab, ga, gb, m, wa_bf, wb_bf, wo_bf,
                              g_post_mix[0].reshape(1, d), g_pre_ffn[0].reshape(1, d),
                              wr_bf, br, cnt_in, TOKEN_BLOCK)

    mod_p, mod_s = mod[:nb_p], mod[nb_p:]
    x1p, h2p, route_p, blk_p, cnt_p = front(x_prompt, mod_p, jnp.zeros((1, LANES), F32))
    x1s, h2s, route_s, blk_s, cnt = front(x_sample, mod_s, cnt_p)

    h2p, h2s = h2p.reshape(-1, d), h2s.reshape(-1, d)
    n_assign = (h2p.shape[0] + h2s.shape[0]) * TOP_K
    tab, tail, tile_expert, n_used, n_sorted = _route_plan(blk_p, blk_s, cnt, n_assign)

    def by_block(route):
        return route[..., :2 * TOP_K].reshape(-1, TOKEN_BLOCK, 2 * TOP_K).transpose(0, 2, 1)

    rt = jnp.concatenate([by_block(route_p), by_block(route_s)], axis=0)
    xs, ws = _dispatch(tab, tail, rt, h2p, h2s, n_sorted)
    ys = _grouped_ffn(tile_expert, n_used, xs, ws, wg_bf, bg, wu_bf, bu, wd_bf, bd)
    y_p = _combine(tab, 0, ys, route_p, x1p, mod_p, g_post_ffn[0])
    y_s = _combine(tab, h2p.shape[0] // TOKEN_BLOCK, ys, route_s, x1s, mod_s, g_post_ffn[0])
    return (y_p, y_s)
```

```python
import functools

import jax
import jax.numpy as jnp
import numpy as np
from jax import lax
from jax.experimental import pallas as pl
from jax.experimental.pallas import tpu as pltpu

D_MODEL = 1024
HEAD_DIM = 64
ROT_DIM = HEAD_DIM // 4
ROPE_THETA = 500000.0
A_HEADS = 8
A_KV_HEADS = 2
WINDOW = 128
A_BLOCK = 128
B_HEADS = 8
GRID_W = 64
NA_ROWS = 8
NA_COLS = 16
N_EXPERTS = 32
TOP_K = 4
SWIGLU_LIMIT = 7.0
SWIGLU_ALPHA = 1.702
RMS_EPS = 1e-6
NEG = -1e30
A_Q = A_HEADS * HEAD_DIM
A_KV = A_KV_HEADS * HEAD_DIM
B_W = B_HEADS * HEAD_DIM

LANES = 128
PAIR = 2 * HEAD_DIM
N_PAIRS = B_W // PAIR
MOE_TILE = 512
TOKEN_BLOCK = 256
ROW_CHUNK_LOG2 = 3
ROW_CHUNK = 1 << ROW_CHUNK_LOG2
LOCAL_ROWS = TOKEN_BLOCK * TOP_K + N_EXPERTS * ROW_CHUNK
LOCAL_CHUNKS = LOCAL_ROWS // ROW_CHUNK
CTAB_BLOCK = 2 * LANES
VMEM_LIMIT = 56 * 1024 * 1024

BF16 = jnp.bfloat16
F32 = jnp.float32


def _cparams(*sem):
    return pltpu.CompilerParams(dimension_semantics=sem, vmem_limit_bytes=VMEM_LIMIT)


def _mod_body(c_ref, w_ref, b_ref, o_ref):
    c = c_ref[...]
    s = c * (1.0 / (1.0 + jnp.exp(-c)))
    o_ref[...] = jnp.dot(s, w_ref[...], preferred_element_type=F32,
                         precision=lax.Precision.HIGHEST) + b_ref[...]


def _modulation(c, ada_w, ada_b):
    n, d = c.shape
    blk = 1024
    return pl.pallas_call(
        _mod_body,
        grid=(ada_w.shape[1] // blk,),
        in_specs=[pl.BlockSpec((n, d), lambda j: (0, 0)),
                  pl.BlockSpec((d, blk), lambda j: (0, j)),
                  pl.BlockSpec((1, blk), lambda j: (0, j))],
        out_specs=pl.BlockSpec((n, blk), lambda j: (0, j)),
        out_shape=jax.ShapeDtypeStruct((n, ada_w.shape[1]), F32),
        compiler_params=_cparams("arbitrary"),
        name="adaln_mod",
    )(c, ada_w, ada_b.reshape(1, -1))


def _rms(x, g):
    return x * lax.rsqrt(jnp.mean(x * x, axis=-1, keepdims=True) + RMS_EPS) * g


def _rope_tile(seg, rope_ref):
    return (seg * rope_ref[0] + pltpu.roll(seg, LANES - ROT_DIM // 2, 1) * rope_ref[1]
            + pltpu.roll(seg, ROT_DIM // 2, 1) * rope_ref[2])


def _inproj_body(x_ref, mod_ref, g_ref, w_ref, rope_ref,
                 qa_ref, ka_ref, va_ref, qb_ref, kb_ref, vb_ref, ga_ref, gb_ref):
    sh1 = mod_ref[0, 0:1, :]
    sc1 = mod_ref[0, 1:2, :]
    h = (_rms(x_ref[0], g_ref[...]) * (1.0 + sc1) + sh1).astype(BF16)

    def proj(lo, hi):
        return jnp.dot(h, w_ref[:, lo:hi], preferred_element_type=F32)

    o = 0
    for t in range(A_Q // LANES):
        qa_ref[0, :, t * LANES:(t + 1) * LANES] = _rope_tile(
            proj(o + t * LANES, o + (t + 1) * LANES), rope_ref).astype(BF16)
    o += A_Q
    ka_ref[0] = _rope_tile(proj(o, o + A_KV), rope_ref).astype(BF16)
    o += A_KV
    va_ref[0] = proj(o, o + A_KV).astype(BF16)
    o += A_KV
    for ref in (qb_ref, kb_ref, vb_ref):
        ref[0] = proj(o, o + B_W).astype(BF16)
        o += B_W
    for ref in (ga_ref, gb_ref):
        ref[0] = proj(o, o + D_MODEL).astype(BF16)
        o += D_MODEL


def _in_projection(x, mod, g_pre, w_in_bf, rope, tm):
    b, s, d = x.shape
    widths = (A_Q, A_KV, A_KV, B_W, B_W, B_W, D_MODEL, D_MODEL)
    tok = lambda w: pl.BlockSpec((1, tm, w), lambda i, j: (i, j, 0))
    return pl.pallas_call(
        _inproj_body,
        grid=(b, s // tm),
        in_specs=[tok(d),
                  pl.BlockSpec((1, 6, d), lambda i, j: (i, 0, 0)),
                  pl.BlockSpec((1, d), lambda i, j: (0, 0)),
                  pl.BlockSpec(w_in_bf.shape, lambda i, j: (0, 0)),
                  pl.BlockSpec((3, tm, LANES), lambda i, j: (0, j, 0))],
        out_specs=[tok(w) for w in widths],
        out_shape=[jax.ShapeDtypeStruct((b, s, w), BF16) for w in widths],
        compiler_params=_cparams("arbitrary", "arbitrary"),
        name="in_projection",
    )(x, mod, g_pre.reshape(1, d), w_in_bf, rope)


def _stack_heads(t):
    lane = lax.broadcasted_iota(jnp.int32, t.shape, 1)
    zero = jnp.zeros_like(t)
    return jnp.concatenate([jnp.where(lane < HEAD_DIM, t, zero),
                            jnp.where(lane >= HEAD_DIM, t, zero)], axis=0)


def _window_body(q_ref, k_ref, v_ref, sink_ref, o_ref, *, seq, tq):
    span = A_BLOCK + 2 * WINDOW
    n_sub = tq // A_BLOCK
    rows = N_PAIRS * A_BLOCK
    qi = lax.broadcasted_iota(jnp.int32, (rows, 2 * span), 0) % A_BLOCK
    kj = lax.broadcasted_iota(jnp.int32, (rows, 2 * span), 1) % span
    lane_o = lax.broadcasted_iota(jnp.int32, (rows, LANES), 1)
    sink = sink_ref[...]
    for sub in range(n_sub):
        q0 = pl.program_id(1) * tq + sub * A_BLOCK
        k0 = pl.multiple_of(jnp.clip(q0 - WINDOW, 0, seq - span), A_BLOCK)
        kst = _stack_heads(k_ref[0, pl.ds(k0, span), :])
        vst = _stack_heads(v_ref[0, pl.ds(k0, span), :])
        qblk = q_ref[0, sub * A_BLOCK:(sub + 1) * A_BLOCK, :]
        qs = jnp.concatenate([qblk[:, p * LANES:(p + 1) * LANES] for p in range(N_PAIRS)], axis=0)
        s = lax.dot_general(qs, kst, (((1,), (1,)), ((), ())), preferred_element_type=F32)
        band = jnp.abs((q0 + qi) - (k0 + kj)) <= WINDOW
        s = jnp.where(band, s, NEG)
        outs = []
        ps = []
        for half in range(A_KV_HEADS):
            sh = s[:, half * span:(half + 1) * span]
            snk = sink[:, half:half + 1]
            m = jnp.maximum(jnp.max(sh, axis=-1, keepdims=True), snk)
            e = jnp.exp(sh - m)
            den = jnp.sum(e, axis=-1, keepdims=True) + jnp.exp(snk - m)
            ps.append(e.astype(BF16))
            outs.append(1.0 / den)
        p = jnp.concatenate(ps, axis=1)
        o = jnp.dot(p, vst, preferred_element_type=F32)
        o = o * jnp.where(lane_o < HEAD_DIM, outs[0], outs[1])
        for pr in range(N_PAIRS):
            o_ref[0, sub * A_BLOCK:(sub + 1) * A_BLOCK, pr * LANES:(pr + 1) * LANES] = (
                o[pr * A_BLOCK:(pr + 1) * A_BLOCK, :].astype(BF16))


def _window_attention(q, k, v, sink_rows, tq):
    b, s, _ = q.shape
    return pl.pallas_call(
        functools.partial(_window_body, seq=s, tq=tq),
        grid=(b, s // tq),
        in_specs=[pl.BlockSpec((1, tq, A_Q), lambda i, j: (i, j, 0)),
                  pl.BlockSpec((1, s, A_KV), lambda i, j: (i, 0, 0)),
                  pl.BlockSpec((1, s, A_KV), lambda i, j: (i, 0, 0)),
                  pl.BlockSpec(sink_rows.shape, lambda i, j: (0, 0))],
        out_specs=pl.BlockSpec((1, tq, A_Q), lambda i, j: (i, j, 0)),
        out_shape=jax.ShapeDtypeStruct((b, s, A_Q), BF16),
        compiler_params=_cparams("arbitrary", "arbitrary"),
        name="window_attention",
    )(q, k, v, sink_rows)


def _nbr_bias_body(rpb_ref, o_ref):
    p = pl.program_id(0)
    d = pl.program_id(1)
    nkeys = NA_ROWS * GRID_W
    qc = lax.broadcasted_iota(jnp.int32, (GRID_W, 2 * nkeys), 0)
    col = lax.broadcasted_iota(jnp.int32, (GRID_W, 2 * nkeys), 1)
    kc = col % GRID_W
    qstart = jnp.clip(qc - NA_COLS // 2, 0, GRID_W - NA_COLS)
    valid = (kc >= qstart) & (kc < qstart + NA_COLS)
    cidx = kc - qc + NA_COLS - 1
    col1 = lax.broadcasted_iota(jnp.int32, (1, 2 * nkeys), 1)
    blk1 = col1 // GRID_W
    acc = jnp.full((GRID_W, 2 * nkeys), NEG, F32)
    for j in range(2 * NA_COLS - 1):
        rv = jnp.zeros((1, 2 * nkeys), F32)
        for hh in range(2):
            for aa in range(NA_ROWS):
                rv = jnp.where(blk1 == hh * NA_ROWS + aa,
                               rpb_ref[2 * p + hh, aa - d + NA_ROWS - 1, j], rv)
        acc = jnp.where(valid & (cidx == j), rv, acc)
    o_ref[0, 0] = acc


def _nbr_bias(rpb):
    nkeys = NA_ROWS * GRID_W
    return pl.pallas_call(
        _nbr_bias_body,
        grid=(N_PAIRS, NA_ROWS),
        in_specs=[pl.BlockSpec(memory_space=pltpu.SMEM)],
        out_specs=pl.BlockSpec((1, 1, GRID_W, 2 * nkeys), lambda p, d: (p, d, 0, 0)),
        out_shape=jax.ShapeDtypeStruct((N_PAIRS, NA_ROWS, GRID_W, 2 * nkeys), F32),
        compiler_params=_cparams("arbitrary", "arbitrary"),
        name="nbr_bias",
    )(rpb)


def _nbr_body(q_ref, k_ref, v_ref, bias_ref, o_ref, *, seq, tq):
    n_rows = seq // GRID_W
    nkeys = NA_ROWS * GRID_W
    lane_o = lax.broadcasted_iota(jnp.int32, (GRID_W, LANES), 1)

    def row(rl, carry):
        r = pl.program_id(1) * (tq // GRID_W) + rl
        r0 = jnp.clip(r - NA_ROWS // 2, 0, n_rows - NA_ROWS)
        d = r - r0
        k0 = pl.multiple_of(r0 * GRID_W, GRID_W)
        q0 = pl.multiple_of(rl * GRID_W, GRID_W)
        for pr in range(N_PAIRS):
            lanes = slice(pr * LANES, (pr + 1) * LANES)
            kst = _stack_heads(k_ref[0, pl.ds(k0, nkeys), lanes])
            vst = _stack_heads(v_ref[0, pl.ds(k0, nkeys), lanes])
            qp = q_ref[0, pl.ds(q0, GRID_W), lanes]
            s = lax.dot_general(qp, kst, (((1,), (1,)), ((), ())), preferred_element_type=F32)
            s = s + bias_ref[pr, d]
            ps = []
            inv = []
            for half in range(2):
                sh = s[:, half * nkeys:(half + 1) * nkeys]
                m = jnp.max(sh, axis=-1, keepdims=True)
                e = jnp.exp(sh - m)
                inv.append(1.0 / jnp.sum(e, axis=-1, keepdims=True))
                ps.append(e.astype(BF16))
            o = jnp.dot(jnp.concatenate(ps, axis=1), vst, preferred_element_type=F32)
            o = o * jnp.where(lane_o < HEAD_DIM, inv[0], inv[1])
            o_ref[0, pl.ds(q0, GRID_W), lanes] = o.astype(BF16)
        return carry

    lax.fori_loop(0, tq // GRID_W, row, 0)


def _nbr_attention(q, k, v, bias, tq):
    b, s, _ = q.shape
    return pl.pallas_call(
        functools.partial(_nbr_body, seq=s, tq=tq),
        grid=(b, s // tq),
        in_specs=[pl.BlockSpec((1, tq, B_W), lambda i, j: (i, j, 0)),
                  pl.BlockSpec((1, s, B_W), lambda i, j: (i, 0, 0)),
                  pl.BlockSpec((1, s, B_W), lambda i, j: (i, 0, 0)),
                  pl.BlockSpec(bias.shape, lambda i, j: (0, 0, 0, 0))],
        out_specs=pl.BlockSpec((1, tq, B_W), lambda i, j: (i, j, 0)),
        out_shape=jax.ShapeDtypeStruct((b, s, B_W), BF16),
        compiler_params=_cparams("arbitrary", "arbitrary"),
        name="nbr_attention",
    )(q, k, v, bias)


def _sigmoid(z):
    return 1.0 / (1.0 + jnp.exp(-z))


def _mix_body(x_ref, aa_ref, ab_ref, ga_ref, gb_ref, mod_ref, wa_ref, wb_ref, wo_ref,
              gpost_ref, gpre_ref, wr_ref, br_ref, cnt_in_ref,
              x1_ref, h2_ref, route_ref, blk_ref, cnt_ref, run_ref, *, tm):
    first = (pl.program_id(0) == 0) & (pl.program_id(1) == 0)

    @pl.when(first)
    def _():
        run_ref[...] = cnt_in_ref[...]

    ya = jnp.dot(aa_ref[0], wa_ref[...], preferred_element_type=F32)
    yb = jnp.dot(ab_ref[0], wb_ref[...], preferred_element_type=F32)
    merged = _sigmoid(ga_ref[0].astype(F32)) * ya + _sigmoid(gb_ref[0].astype(F32)) * yb
    z = jnp.dot(merged.astype(BF16), wo_ref[...], preferred_element_type=F32)
    gt1 = mod_ref[0, 2:3, :]
    sh2 = mod_ref[0, 3:4, :]
    sc2 = mod_ref[0, 4:5, :]
    x1 = x_ref[0] + gt1 * _rms(z, gpost_ref[...])
    x1_ref[0] = x1
    h2 = _rms(x1, gpre_ref[...]) * (1.0 + sc2) + sh2
    h2_bf = h2.astype(BF16)
    h2_ref[0] = h2_bf

    logits = jnp.dot(h2_bf, wr_ref[...], preferred_element_type=F32) + br_ref[...]
    lane = lax.broadcasted_iota(jnp.int32, (tm, LANES), 1).astype(F32)
    work = logits
    sel = jnp.zeros((tm, LANES), F32)
    vals, idxs = [], []
    for _ in range(TOP_K):
        m = jnp.max(work, axis=-1, keepdims=True)
        idx = jnp.min(jnp.where(work == m, lane, float(LANES)), axis=-1, keepdims=True)
        hit = lane == idx
        vals.append(m)
        idxs.append(idx)
        work = jnp.where(hit, -jnp.inf, work)
        sel = sel + hit.astype(F32)
    es = [jnp.exp(v - vals[0]) for v in vals]
    den = es[0] + es[1] + es[2] + es[3]

    ri = lax.broadcasted_iota(jnp.int32, (tm, tm), 0)
    ci = lax.broadcasted_iota(jnp.int32, (tm, tm), 1)
    tri = (ci < ri).astype(BF16)
    earlier = jnp.dot(tri, sel.astype(BF16), preferred_element_type=F32)
    n_blk = jnp.sum(sel, axis=0, keepdims=True)
    chunks = jnp.floor((n_blk + (ROW_CHUNK - 1.0)) * (1.0 / ROW_CHUNK))
    li = lax.broadcasted_iota(jnp.int32, (LANES, LANES), 0)
    lj = lax.broadcasted_iota(jnp.int32, (LANES, LANES), 1)
    upper = (li < lj).astype(BF16)
    seg_start = ROW_CHUNK * jnp.dot(jnp.broadcast_to(chunks, (8, LANES)).astype(BF16), upper,
                                    preferred_element_type=F32)[0:1, :]
    slot = earlier + seg_start
    out = jnp.zeros((tm, LANES), F32)
    for k in range(TOP_K):
        local = jnp.sum(jnp.where(lane == idxs[k], slot, 0.0), axis=-1, keepdims=True)
        out = jnp.where(lane == k, es[k] / den, out)
        out = jnp.where(lane == TOP_K + k, local, out)
    route_ref[0] = out
    row = lax.broadcasted_iota(jnp.int32, (8, LANES), 0)
    blk_ref[0] = jnp.where(row == 0, n_blk, jnp.where(row == 1, run_ref[...],
                                                     jnp.where(row == 2, seg_start, 0.0)))
    run_ref[...] = run_ref[...] + ROW_CHUNK * chunks
    cnt_ref[...] = run_ref[...]


def _mix_and_route(x, aa, ab, ga, gb, mod, wa, wb, wo, gpost, gpre, wr, br, cnt_in, tm):
    b, s, d = x.shape
    tok = lambda w: pl.BlockSpec((1, tm, w), lambda i, j: (i, j, 0))
    full = lambda a: pl.BlockSpec(a.shape, lambda i, j: (0,) * a.ndim)
    return pl.pallas_call(
        functools.partial(_mix_body, tm=tm),
        grid=(b, s // tm),
        in_specs=[tok(d), tok(A_Q), tok(B_W), tok(d), tok(d),
                  pl.BlockSpec((1, 6, d), lambda i, j: (i, 0, 0)),
                  full(wa), full(wb), full(wo), full(gpost), full(gpre), full(wr), full(br),
                  full(cnt_in)],
        out_specs=[tok(d), tok(d), tok(LANES), pl.BlockSpec((1, 8, LANES), lambda i, j: (i, j, 0)),
                   pl.BlockSpec((1, LANES), lambda i, j: (0, 0))],
        out_shape=[jax.ShapeDtypeStruct((b, s, d), F32), jax.ShapeDtypeStruct((b, s, d), BF16),
                   jax.ShapeDtypeStruct((b, s, LANES), F32),
                   jax.ShapeDtypeStruct((b, (s // tm) * 8, LANES), F32),
                   jax.ShapeDtypeStruct((1, LANES), F32)],
        scratch_shapes=[pltpu.VMEM((1, LANES), F32)],
        compiler_params=_cparams("arbitrary", "arbitrary"),
        name="mix_and_route",
    )(x, aa, ab, ga, gb, mod, wa, wb, wo, gpost, gpre, wr, br, cnt_in)


def _chunk_loop(fn):
    def body(j, carry):
        fn(j)
        return carry

    lax.fori_loop(0, LOCAL_CHUNKS, body, 0, unroll=8)


def _dispatch_body(ctab_ref, tail_ref, rt_ref, hp_ref, hs_ref, xs_ref, loc_ref, zero_ref, sem,
                   *, np_steps, n_steps):
    i = pl.program_id(0)
    slot = i % 2
    d = D_MODEL

    def full_wait(s):
        pltpu.make_async_copy(loc_ref.at[s], xs_ref.at[pl.ds(0, LOCAL_ROWS)], sem.at[s]).wait()

    @pl.when(i == 0)
    def _():
        zero_ref[...] = jnp.zeros_like(zero_ref)

        def chunk_copy(s):
            return pltpu.make_async_copy(zero_ref.at[pl.ds(0, ROW_CHUNK)],
                                         xs_ref.at[pl.ds(s, ROW_CHUNK)], sem.at[0])

        def tails(fn):
            def per_expert(e, carry):
                def chunk(c, carry2):
                    fn(chunk_copy(pl.multiple_of(tail_ref[e] + c * ROW_CHUNK, ROW_CHUNK)))
                    return carry2

                lax.fori_loop(0, tail_ref[N_EXPERTS + e], chunk, 0)
                return carry

            lax.fori_loop(0, N_EXPERTS, per_expert, 0)

        tails(lambda cp: cp.start())
        tails(lambda cp: cp.wait())

        def unused(fn):
            def tile(t, carry):
                r = pl.multiple_of(t * MOE_TILE, MOE_TILE)
                fn(pltpu.make_async_copy(zero_ref, xs_ref.at[pl.ds(r, MOE_TILE)], sem.at[0]))
                return carry

            lax.fori_loop(tail_ref[2 * N_EXPERTS], xs_ref.shape[0] // MOE_TILE, tile, 0)

        unused(lambda cp: cp.start())
        unused(lambda cp: cp.wait())

    @pl.when(i >= 2)
    def _():
        full_wait(slot)

    rt = rt_ref[0]
    srow = lax.broadcasted_iota(jnp.int32, (LOCAL_ROWS, TOKEN_BLOCK), 0).astype(F32)
    hit_any = None
    wsel = jnp.zeros((LOCAL_ROWS, TOKEN_BLOCK), F32)
    for k in range(TOP_K):
        hit = srow == rt[TOP_K + k:TOP_K + k + 1, :]
        wsel = jnp.where(hit, rt[k:k + 1, :], wsel)
        hit_any = hit if hit_any is None else (hit_any | hit)
    onehot = jnp.where(hit_any, 1.0, 0.0).astype(BF16)

    @pl.when(i < np_steps)
    def _():
        loc_ref[slot, :, 0:d] = jnp.dot(onehot, hp_ref[...], preferred_element_type=F32)

    @pl.when(i >= np_steps)
    def _():
        loc_ref[slot, :, 0:d] = jnp.dot(onehot, hs_ref[...], preferred_element_type=F32)

    loc_ref[slot, :, d:d + LANES] = jnp.broadcast_to(jnp.sum(wsel, axis=1, keepdims=True),
                                                     (LOCAL_ROWS, LANES))

    def start(j):
        pltpu.make_async_copy(
            loc_ref.at[slot, pl.ds(pl.multiple_of(j * ROW_CHUNK, ROW_CHUNK), ROW_CHUNK)],
            xs_ref.at[pl.ds(pl.multiple_of(ctab_ref[j], ROW_CHUNK), ROW_CHUNK)], sem.at[slot]).start()

    _chunk_loop(start)

    @pl.when(i == n_steps - 1)
    def _():
        full_wait(slot)
        if n_steps > 1:
            full_wait(1 - slot)


def _dispatch(ctab, tail, rt, h2p, h2s, n_sorted):
    d = h2p.shape[1]
    np_steps = h2p.shape[0] // TOKEN_BLOCK
    ns_steps = h2s.shape[0] // TOKEN_BLOCK
    return pl.pallas_call(
        functools.partial(_dispatch_body, np_steps=np_steps, n_steps=np_steps + ns_steps),
        grid=(np_steps + ns_steps,),
        in_specs=[pl.BlockSpec((CTAB_BLOCK,), lambda i: (i,), memory_space=pltpu.SMEM),
                  pl.BlockSpec((LANES,), lambda i: (0,), memory_space=pltpu.SMEM),
                  pl.BlockSpec((1, 8, TOKEN_BLOCK), lambda i: (i, 0, 0)),
                  pl.BlockSpec((TOKEN_BLOCK, d), lambda i: (jnp.minimum(i, np_steps - 1), 0)),
                  pl.BlockSpec((TOKEN_BLOCK, d),
                               lambda i: (jnp.clip(i - np_steps, 0, ns_steps - 1), 0))],
        out_specs=pl.BlockSpec(memory_space=pl.ANY),
        out_shape=jax.ShapeDtypeStruct((n_sorted, d + LANES), F32),
        scratch_shapes=[pltpu.VMEM((2, LOCAL_ROWS, d + LANES), F32),
                        pltpu.VMEM((MOE_TILE, d + LANES), F32),
                        pltpu.SemaphoreType.DMA((2,))],
        compiler_params=_cparams("arbitrary"),
        name="moe_dispatch",
    )(ctab, tail, rt, h2p, h2s)


def _gmm_body(te_ref, nu_ref, x_ref, wg_ref, bg_ref, wu_ref, bu_ref, wd_ref, bd_ref, y_ref):
    i = pl.program_id(0)
    d = D_MODEL

    @pl.when(i < nu_ref[0])
    def _():
        x = x_ref[:, 0:d].astype(BF16)
        g = jnp.minimum(jnp.dot(x, wg_ref[0], preferred_element_type=F32) + bg_ref[0], SWIGLU_LIMIT)
        u = jnp.clip(jnp.dot(x, wu_ref[0], preferred_element_type=F32) + bu_ref[0],
                     -SWIGLU_LIMIT, SWIGLU_LIMIT)
        act = (u + 1.0) * (g * _sigmoid(SWIGLU_ALPHA * g))
        y = jnp.dot(act.astype(BF16), wd_ref[0], preferred_element_type=F32) + bd_ref[0]
        y_ref[...] = x_ref[:, d:d + 1] * y

    @pl.when(i >= nu_ref[0])
    def _():
        y_ref[...] = jnp.zeros_like(y_ref)


def _grouped_ffn(tile_expert, n_used, xs, wg, bg, wu, bu, wd, bd):
    n = xs.shape[0]
    d, f = wg.shape[1], wg.shape[2]
    wspec = lambda a, b_: pl.BlockSpec((1, a, b_), lambda i, te, nu: (te[i], 0, 0))
    return pl.pallas_call(
        _gmm_body,
        grid_spec=pltpu.PrefetchScalarGridSpec(
            num_scalar_prefetch=2,
            grid=(n // MOE_TILE,),
            in_specs=[pl.BlockSpec((MOE_TILE, d + LANES),
                                   lambda i, te, nu: (jnp.minimum(i, nu[0] - 1), 0)),
                      wspec(d, f), wspec(1, f), wspec(d, f), wspec(1, f), wspec(f, d), wspec(1, d)],
            out_specs=pl.BlockSpec((MOE_TILE, d), lambda i, te, nu: (i, 0)),
        ),
        out_shape=jax.ShapeDtypeStruct((n, d), F32),
        compiler_params=_cparams("arbitrary"),
        name="moe_grouped_ffn",
    )(tile_expert, n_used, xs, wg, bg, wu, bu, wd, bd)


def _combine_body(ctab_ref, cnext_ref, ys_ref, route_ref, x1_ref, mod_ref, g_ref, o_ref, loc_ref, sem,
                  *, n_steps):
    t = pl.program_id(0) * pl.num_programs(1) + pl.program_id(1)
    slot = t % 2

    def fetch(tab_ref, s):
        def start(j):
            pltpu.make_async_copy(
                ys_ref.at[pl.ds(pl.multiple_of(tab_ref[j], ROW_CHUNK), ROW_CHUNK)],
                loc_ref.at[s, pl.ds(pl.multiple_of(j * ROW_CHUNK, ROW_CHUNK), ROW_CHUNK)],
                sem.at[s]).start()

        _chunk_loop(start)

    @pl.when(t == 0)
    def _():
        fetch(ctab_ref, slot)

    @pl.when(t + 1 < n_steps)
    def _():
        fetch(cnext_ref, 1 - slot)

    pltpu.make_async_copy(ys_ref.at[pl.ds(0, LOCAL_ROWS)], loc_ref.at[slot], sem.at[slot]).wait()

    route = route_ref[0]
    col = lax.broadcasted_iota(jnp.int32, (TOKEN_BLOCK, LOCAL_ROWS), 1).astype(F32)
    hit_any = None
    for k in range(TOP_K):
        hit = col == route[:, TOP_K + k:TOP_K + k + 1]
        hit_any = hit if hit_any is None else (hit_any | hit)
    onehot = jnp.where(hit_any, 1.0, 0.0).astype(BF16)
    y2 = jnp.dot(onehot, loc_ref[slot].astype(BF16), preferred_element_type=F32)
    gt2 = mod_ref[0, 5:6, :]
    o_ref[0] = x1_ref[0] + gt2 * _rms(y2, g_ref[...])


def _combine(ctab, first_block, ys, route, x1, mod, g_post):
    b, s, d = x1.shape
    spb = s // TOKEN_BLOCK
    n_steps = b * spb
    tok = lambda w: pl.BlockSpec((1, TOKEN_BLOCK, w), lambda i, j: (i, j, 0))
    blk = lambda i, j: first_block + i * spb + j
    return pl.pallas_call(
        functools.partial(_combine_body, n_steps=n_steps),
        grid=(b, spb),
        in_specs=[pl.BlockSpec((CTAB_BLOCK,), lambda i, j: (blk(i, j),), memory_space=pltpu.SMEM),
                  pl.BlockSpec((CTAB_BLOCK,),
                               lambda i, j: (jnp.minimum(blk(i, j) + 1, first_block + n_steps - 1),),
                               memory_space=pltpu.SMEM),
                  pl.BlockSpec(memory_space=pl.ANY),
                  tok(LANES), tok(d),
                  pl.BlockSpec((1, 6, d), lambda i, j: (i, 0, 0)),
                  pl.BlockSpec((1, d), lambda i, j: (0, 0))],
        out_specs=tok(d),
        out_shape=jax.ShapeDtypeStruct((b, s, d), F32),
        scratch_shapes=[pltpu.VMEM((2, LOCAL_ROWS, d), F32), pltpu.SemaphoreType.DMA((2,))],
        compiler_params=_cparams("arbitrary", "arbitrary"),
        name="moe_combine",
    )(ctab, ctab, ys, route, x1, mod, g_post.reshape(1, d))


def _rope_tables(s_max):
    half = ROT_DIM // 2
    inv = jnp.power(jnp.float32(ROPE_THETA), -jnp.arange(0, ROT_DIM, 2, dtype=F32) / ROT_DIM)
    ang = jnp.arange(s_max, dtype=F32)[:, None] * inv[None, :]
    cos, sin = jnp.cos(ang), jnp.sin(ang)
    d = np.arange(LANES) % HEAD_DIM
    lo = jnp.asarray(d < half)
    hi = jnp.asarray((d >= half) & (d < ROT_DIM))
    cos_l = cos[:, d % half]
    sin_l = sin[:, d % half]
    c = jnp.where(lo | hi, cos_l, 1.0)
    s1 = jnp.where(lo, -sin_l, 0.0)
    s2 = jnp.where(hi, sin_l, 0.0)
    return jnp.stack([c, s1, s2]).astype(F32)


def _route_plan(blk_p, blk_s, counts, n_assign):
    blk = jnp.concatenate([blk_p.reshape(-1, 8, LANES), blk_s.reshape(-1, 8, LANES)], axis=0)
    n_blk = blk[:, 0, :N_EXPERTS].astype(jnp.int32)
    before = blk[:, 1, :N_EXPERTS].astype(jnp.int32)
    local0 = blk[:, 2, :N_EXPERTS].astype(jnp.int32)
    cnt = counts[0, :N_EXPERTS].astype(jnp.int32)
    tiles = (cnt + (MOE_TILE - 1)) // MOE_TILE
    tile_end = jnp.cumsum(tiles)
    start = (tile_end - tiles) * MOE_TILE
    n_pad = blk.shape[0] * N_EXPERTS * (ROW_CHUNK - 1)
    n_bound = (n_assign + n_pad) // MOE_TILE + N_EXPERTS
    n_tiles = n_bound + (2 * LOCAL_ROWS + MOE_TILE - 1) // MOE_TILE
    experts = jnp.arange(N_EXPERTS, dtype=jnp.int32)
    chunks = (n_blk + (ROW_CHUNK - 1)) // ROW_CHUNK
    chunk_end = jnp.cumsum(chunks, axis=1)
    j = jnp.arange(LOCAL_CHUNKS, dtype=jnp.int32)
    e = jnp.sum(chunk_end[:, None, :] <= j[None, :, None], axis=-1)
    pick = lambda table: jnp.sum(jnp.where(e[..., None] == experts, table[:, None, :], 0), axis=-1)
    in_segment = pick(start[None, :] + before) + ROW_CHUNK * (j[None, :] - pick(chunk_end - chunks))
    parity = (jnp.arange(blk.shape[0], dtype=jnp.int32) % 2)[:, None]
    scratch_row = n_bound * MOE_TILE + parity * LOCAL_ROWS + ROW_CHUNK * j[None, :]
    ctab = jnp.where(e < N_EXPERTS, in_segment, scratch_row)
    ctab = jnp.pad(ctab, ((0, 0), (0, CTAB_BLOCK - LOCAL_CHUNKS))).reshape(-1).astype(jnp.int32)
    tile_id = jnp.arange(n_tiles, dtype=jnp.int32)
    tile_expert = jnp.minimum(jnp.sum(tile_end[None, :] <= tile_id[:, None], axis=-1),
                              N_EXPERTS - 1).astype(jnp.int32)
    n_used = tile_end[-1:].astype(jnp.int32)
    zero0 = start + cnt
    zero_chunks = (tile_end * MOE_TILE - zero0) // ROW_CHUNK
    tail = jnp.concatenate([zero0, zero_chunks, n_used,
                            jnp.zeros((LANES - 2 * N_EXPERTS - 1,), jnp.int32)])
    return ctab, tail.astype(jnp.int32), tile_expert, n_used, n_tiles * MOE_TILE


def kernel(x_prompt, x_sample, c_prompt, c_sample, ada_w, ada_b, g_pre_mix, g_post_mix, g_pre_ffn,
           g_post_ffn, w_in, sink_a, rpb_b, w_branch_a, w_branch_b, w_out, w_router, b_router,
           w_gate, b_gate, w_up, b_up, w_down, b_down):
    d = D_MODEL
    scale = HEAD_DIM ** -0.5
    g = A_HEADS // A_KV_HEADS
    w = w_in[0]
    col_scale = np.ones((w.shape[1],), np.float32)
    col_scale[:A_Q] = scale
    qb0 = A_Q + 2 * A_KV
    col_scale[qb0:qb0 + B_W] = scale
    wq = w[:, :A_Q].reshape(d, A_KV_HEADS, g, HEAD_DIM).transpose(0, 2, 1, 3).reshape(d, A_Q)
    w_in_bf = (jnp.concatenate([wq, w[:, A_Q:]], axis=1) * col_scale).astype(BF16)
    wa_bf = w_branch_a[0].reshape(A_KV_HEADS, g, HEAD_DIM, d).transpose(1, 0, 2, 3).reshape(
        A_Q, d).astype(BF16)
    wb_bf = w_branch_b[0].astype(BF16)
    wo_bf = w_out[0].astype(BF16)
    wr_bf = jnp.zeros((d, LANES), F32).at[:, :N_EXPERTS].set(w_router[0]).astype(BF16)
    br = jnp.full((1, LANES), NEG, F32).at[0, :N_EXPERTS].set(b_router[0])
    wg_bf, wu_bf, wd_bf = w_gate[0].astype(BF16), w_up[0].astype(BF16), w_down[0].astype(BF16)
    bg, bu, bd = (t[0].reshape(N_EXPERTS, 1, -1) for t in (b_gate, b_up, b_down))
    sink_rows = jnp.repeat(sink_a[0].reshape(A_KV_HEADS, g).T, A_BLOCK, axis=0).astype(F32)

    nb_p = x_prompt.shape[0]
    mod = _modulation(jnp.concatenate([c_prompt, c_sample], axis=0), ada_w[0], ada_b[0])
    mod = mod.reshape(-1, 6, d)
    s_max = max(x_prompt.shape[1], x_sample.shape[1])
    rope = _rope_tables(s_max)
    bias = _nbr_bias(rpb_b[0])

    def front(x, m, cnt_in):
        qa, ka, va, qb, kb, vb, ga, gb = _in_projection(x, m, g_pre_mix[0], w_in_bf, rope, 512)
        aa = _window_attention(qa, ka, va, sink_rows, 512)
        ab = _nbr_attention(qb, kb, vb, bias, 512)
        return _mix_and_route(x, aa, ab, ga, gb, m, wa_bf, wb_bf, wo_bf,
                              g_post_mix[0].reshape(1, d), g_pre_ffn[0].reshape(1, d),
                              wr_bf, br, cnt_in, TOKEN_BLOCK)

    mod_p, mod_s = mod[:nb_p], mod[nb_p:]
    x1p, h2p, route_p, blk_p, cnt_p = front(x_prompt, mod_p, jnp.zeros((1, LANES), F32))
    x1s, h2s, route_s, blk_s, cnt = front(x_sample, mod_s, cnt_p)

    h2p, h2s = h2p.reshape(-1, d), h2s.reshape(-1, d)
    n_assign = (h2p.shape[0] + h2s.shape[0]) * TOP_K
    ctab, tail, tile_expert, n_used, n_sorted = _route_plan(blk_p, blk_s, cnt, n_assign)

    def by_block(route):
        return route[..., :2 * TOP_K].reshape(-1, TOKEN_BLOCK, 2 * TOP_K).transpose(0, 2, 1)

    rt = jnp.concatenate([by_block(route_p), by_block(route_s)], axis=0)
    xs = _dispatch(ctab, tail, rt, h2p, h2s, n_sorted)
    ys = _grouped_ffn(tile_expert, n_used, xs, wg_bf, bg, wu_bf, bu, wd_bf, bd)
    y_p = _combine(ctab, 0, ys, route_p, x1p, mod_p, g_post_ffn[0])
    y_s = _combine(ctab, h2p.shape[0] // TOKEN_BLOCK, ys, route_s, x1s, mod_s, g_post_ffn[0])
    return (y_p, y_s)
```

```python
import functools

import jax
import jax.numpy as jnp
import numpy as np
from jax import lax
from jax.experimental import pallas as pl
from jax.experimental.pallas import tpu as pltpu

D_MODEL = 1024
HEAD_DIM = 64
ROT_DIM = HEAD_DIM // 4
ROPE_THETA = 500000.0
A_HEADS = 8
A_KV_HEADS = 2
WINDOW = 128
A_BLOCK = 128
B_HEADS = 8
GRID_W = 64
NA_ROWS = 8
NA_COLS = 16
N_EXPERTS = 32
TOP_K = 4
SWIGLU_LIMIT = 7.0
SWIGLU_ALPHA = 1.702
RMS_EPS = 1e-6
NEG = -1e30
A_Q = A_HEADS * HEAD_DIM
A_KV = A_KV_HEADS * HEAD_DIM
B_W = B_HEADS * HEAD_DIM

LANES = 128
PAIR = 2 * HEAD_DIM
N_PAIRS = B_W // PAIR
MOE_TILE = 512
TOKEN_BLOCK = 256
ROW_CHUNK_LOG2 = 3
ROW_CHUNK = 1 << ROW_CHUNK_LOG2
LOCAL_ROWS = TOKEN_BLOCK * TOP_K + N_EXPERTS * ROW_CHUNK
LOCAL_CHUNKS = LOCAL_ROWS // ROW_CHUNK
CTAB_BLOCK = 2 * LANES
VMEM_LIMIT = 56 * 1024 * 1024

BF16 = jnp.bfloat16
F32 = jnp.float32


def _cparams(*sem):
    return pltpu.CompilerParams(dimension_semantics=sem, vmem_limit_bytes=VMEM_LIMIT)


def _mod_body(c_ref, w_ref, b_ref, o_ref):
    c = c_ref[...]
    s = c * (1.0 / (1.0 + jnp.exp(-c)))
    o_ref[...] = jnp.dot(s, w_ref[...], preferred_element_type=F32,
                         precision=lax.Precision.HIGHEST) + b_ref[...]


def _modulation(c, ada_w, ada_b):
    n, d = c.shape
    blk = 1024
    return pl.pallas_call(
        _mod_body,
        grid=(ada_w.shape[1] // blk,),
        in_specs=[pl.BlockSpec((n, d), lambda j: (0, 0)),
                  pl.BlockSpec((d, blk), lambda j: (0, j)),
                  pl.BlockSpec((1, blk), lambda j: (0, j))],
        out_specs=pl.BlockSpec((n, blk), lambda j: (0, j)),
        out_shape=jax.ShapeDtypeStruct((n, ada_w.shape[1]), F32),
        compiler_params=_cparams("arbitrary"),
        name="adaln_mod",
    )(c, ada_w, ada_b.reshape(1, -1))


def _rms(x, g):
    return x * lax.rsqrt(jnp.mean(x * x, axis=-1, keepdims=True) + RMS_EPS) * g


def _rope_tile(seg, rope_ref):
    return (seg * rope_ref[0] + pltpu.roll(seg, LANES - ROT_DIM // 2, 1) * rope_ref[1]
            + pltpu.roll(seg, ROT_DIM // 2, 1) * rope_ref[2])


def _inproj_body(x_ref, mod_ref, g_ref, w_ref, rope_ref,
                 qa_ref, ka_ref, va_ref, qb_ref, kb_ref, vb_ref, ga_ref, gb_ref):
    sh1 = mod_ref[0, 0:1, :]
    sc1 = mod_ref[0, 1:2, :]
    h = (_rms(x_ref[0], g_ref[...]) * (1.0 + sc1) + sh1).astype(BF16)

    def proj(lo, hi):
        return jnp.dot(h, w_ref[:, lo:hi], preferred_element_type=F32)

    o = 0
    for t in range(A_Q // LANES):
        qa_ref[0, :, t * LANES:(t + 1) * LANES] = _rope_tile(
            proj(o + t * LANES, o + (t + 1) * LANES), rope_ref).astype(BF16)
    o += A_Q
    ka_ref[0] = _rope_tile(proj(o, o + A_KV), rope_ref).astype(BF16)
    o += A_KV
    va_ref[0] = proj(o, o + A_KV).astype(BF16)
    o += A_KV
    for ref in (qb_ref, kb_ref, vb_ref):
        ref[0] = proj(o, o + B_W).astype(BF16)
        o += B_W
    for ref in (ga_ref, gb_ref):
        ref[0] = proj(o, o + D_MODEL).astype(BF16)
        o += D_MODEL


def _in_projection(x, mod, g_pre, w_in_bf, rope, tm):
    b, s, d = x.shape
    widths = (A_Q, A_KV, A_KV, B_W, B_W, B_W, D_MODEL, D_MODEL)
    tok = lambda w: pl.BlockSpec((1, tm, w), lambda i, j: (i, j, 0))
    return pl.pallas_call(
        _inproj_body,
        grid=(b, s // tm),
        in_specs=[tok(d),
                  pl.BlockSpec((1, 6, d), lambda i, j: (i, 0, 0)),
                  pl.BlockSpec((1, d), lambda i, j: (0, 0)),
                  pl.BlockSpec(w_in_bf.shape, lambda i, j: (0, 0)),
                  pl.BlockSpec((3, tm, LANES), lambda i, j: (0, j, 0))],
        out_specs=[tok(w) for w in widths],
        out_shape=[jax.ShapeDtypeStruct((b, s, w), BF16) for w in widths],
        compiler_params=_cparams("arbitrary", "arbitrary"),
        name="in_projection",
    )(x, mod, g_pre.reshape(1, d), w_in_bf, rope)


def _stack_heads(t):
    lane = lax.broadcasted_iota(jnp.int32, t.shape, 1)
    zero = jnp.zeros_like(t)
    return jnp.concatenate([jnp.where(lane < HEAD_DIM, t, zero),
                            jnp.where(lane >= HEAD_DIM, t, zero)], axis=0)


def _window_body(q_ref, k_ref, v_ref, sink_ref, band_ref, o_ref, *, seq, tq):
    span = A_BLOCK + 2 * WINDOW
    rows = N_PAIRS * A_BLOCK
    lane_o = lax.broadcasted_iota(jnp.int32, (rows, LANES), 1)
    sink = sink_ref[...]
    for sub in range(tq // A_BLOCK):
        q0 = pl.program_id(1) * tq + sub * A_BLOCK
        k0 = pl.multiple_of(jnp.clip(q0 - WINDOW, 0, seq - span), A_BLOCK)
        kst = _stack_heads(k_ref[0, pl.ds(k0, span), :])
        vst = _stack_heads(v_ref[0, pl.ds(k0, span), :])
        qblk = q_ref[0, sub * A_BLOCK:(sub + 1) * A_BLOCK, :]
        qs = jnp.concatenate([qblk[:, p * LANES:(p + 1) * LANES] for p in range(N_PAIRS)], axis=0)
        s = lax.dot_general(qs, kst, (((1,), (1,)), ((), ())), preferred_element_type=F32)
        s = s + band_ref[lax.div(q0 - k0, A_BLOCK)]
        outs = []
        ps = []
        for half in range(A_KV_HEADS):
            sh = s[:, half * span:(half + 1) * span]
            snk = sink[:, half:half + 1]
            m = jnp.maximum(jnp.max(sh, axis=-1, keepdims=True), snk)
            e = jnp.exp(sh - m)
            den = jnp.sum(e, axis=-1, keepdims=True) + jnp.exp(snk - m)
            ps.append(e.astype(BF16))
            outs.append(1.0 / den)
        p = jnp.concatenate(ps, axis=1)
        o = jnp.dot(p, vst, preferred_element_type=F32)
        o = o * jnp.where(lane_o < HEAD_DIM, outs[0], outs[1])
        for pr in range(N_PAIRS):
            o_ref[0, sub * A_BLOCK:(sub + 1) * A_BLOCK, pr * LANES:(pr + 1) * LANES] = (
                o[pr * A_BLOCK:(pr + 1) * A_BLOCK, :].astype(BF16))


def _window_attention(q, k, v, sink_rows, tq):
    b, s, _ = q.shape
    span = A_BLOCK + 2 * WINDOW
    off = np.arange(3)[:, None, None] * A_BLOCK
    qi = (np.arange(N_PAIRS * A_BLOCK) % A_BLOCK)[None, :, None]
    kj = (np.arange(A_KV_HEADS * span) % span)[None, None, :]
    band = jnp.asarray(np.where(np.abs(qi + off - kj) <= WINDOW, 0.0, NEG).astype(np.float32))
    return pl.pallas_call(
        functools.partial(_window_body, seq=s, tq=tq),
        grid=(b, s // tq),
        in_specs=[pl.BlockSpec((1, tq, A_Q), lambda i, j: (i, j, 0)),
                  pl.BlockSpec((1, s, A_KV), lambda i, j: (i, 0, 0)),
                  pl.BlockSpec((1, s, A_KV), lambda i, j: (i, 0, 0)),
                  pl.BlockSpec(sink_rows.shape, lambda i, j: (0, 0)),
                  pl.BlockSpec(band.shape, lambda i, j: (0, 0, 0))],
        out_specs=pl.BlockSpec((1, tq, A_Q), lambda i, j: (i, j, 0)),
        out_shape=jax.ShapeDtypeStruct((b, s, A_Q), BF16),
        compiler_params=_cparams("arbitrary", "arbitrary"),
        name="window_attention",
    )(q, k, v, sink_rows, band)


def _nbr_bias_body(rpb_ref, o_ref):
    p = pl.program_id(0)
    d = pl.program_id(1)
    nkeys = NA_ROWS * GRID_W
    row = lax.broadcasted_iota(jnp.int32, (2 * GRID_W, nkeys), 0)
    col = lax.broadcasted_iota(jnp.int32, (2 * GRID_W, nkeys), 1)
    qc = row % GRID_W
    kc = col % GRID_W
    qstart = jnp.clip(qc - NA_COLS // 2, 0, GRID_W - NA_COLS)
    valid = (kc >= qstart) & (kc < qstart + NA_COLS)
    cidx = kc - qc + NA_COLS - 1
    key_row = lax.broadcasted_iota(jnp.int32, (1, nkeys), 1) // GRID_W
    acc = jnp.full((2 * GRID_W, nkeys), NEG, F32)
    for j in range(2 * NA_COLS - 1):
        rvs = []
        for hh in range(2):
            rv = jnp.zeros((1, nkeys), F32)
            for aa in range(NA_ROWS):
                rv = jnp.where(key_row == aa, rpb_ref[2 * p + hh, aa - d + NA_ROWS - 1, j], rv)
            rvs.append(rv)
        acc = jnp.where(valid & (cidx == j), jnp.where(row < GRID_W, rvs[0], rvs[1]), acc)
    o_ref[0, 0] = acc.T


def _nbr_bias(rpb):
    nkeys = NA_ROWS * GRID_W
    return pl.pallas_call(
        _nbr_bias_body,
        grid=(N_PAIRS, NA_ROWS),
        in_specs=[pl.BlockSpec(memory_space=pltpu.SMEM)],
        out_specs=pl.BlockSpec((1, 1, nkeys, 2 * GRID_W), lambda p, d: (p, d, 0, 0)),
        out_shape=jax.ShapeDtypeStruct((N_PAIRS, NA_ROWS, nkeys, 2 * GRID_W), F32),
        compiler_params=_cparams("arbitrary", "arbitrary"),
        name="nbr_bias",
    )(rpb)


def _nbr_body(q_ref, k_ref, v_ref, bias_ref, o_ref, *, seq, tq):
    n_rows = seq // GRID_W
    nkeys = NA_ROWS * GRID_W
    lane_o = lax.broadcasted_iota(jnp.int32, (GRID_W, LANES), 1)

    def row(rl, carry):
        r = pl.program_id(1) * (tq // GRID_W) + rl
        r0 = jnp.clip(r - NA_ROWS // 2, 0, n_rows - NA_ROWS)
        d = r - r0
        k0 = pl.multiple_of(r0 * GRID_W, GRID_W)
        q0 = pl.multiple_of(rl * GRID_W, GRID_W)
        for pr in range(N_PAIRS):
            lanes = slice(pr * LANES, (pr + 1) * LANES)
            kw = k_ref[0, pl.ds(k0, nkeys), lanes]
            vw = v_ref[0, pl.ds(k0, nkeys), lanes]
            qs = _stack_heads(q_ref[0, pl.ds(q0, GRID_W), lanes])
            st = lax.dot_general(kw, qs, (((1,), (1,)), ((), ())), preferred_element_type=F32)
            st = st + bias_ref[pr, d]
            m = jnp.max(st, axis=0, keepdims=True)
            e = jnp.exp(st - m)
            inv = 1.0 / jnp.sum(e, axis=0, keepdims=True)
            o = lax.dot_general((e * inv).astype(BF16), vw, (((0,), (0,)), ((), ())),
                                preferred_element_type=F32)
            o_ref[0, pl.ds(q0, GRID_W), lanes] = jnp.where(
                lane_o < HEAD_DIM, o[:GRID_W], o[GRID_W:]).astype(BF16)
        return carry

    lax.fori_loop(0, tq // GRID_W, row, 0, unroll=2)


def _nbr_attention(q, k, v, bias, tq):
    b, s, _ = q.shape
    return pl.pallas_call(
        functools.partial(_nbr_body, seq=s, tq=tq),
        grid=(b, s // tq),
        in_specs=[pl.BlockSpec((1, tq, B_W), lambda i, j: (i, j, 0)),
                  pl.BlockSpec((1, s, B_W), lambda i, j: (i, 0, 0)),
                  pl.BlockSpec((1, s, B_W), lambda i, j: (i, 0, 0)),
                  pl.BlockSpec(bias.shape, lambda i, j: (0, 0, 0, 0))],
        out_specs=pl.BlockSpec((1, tq, B_W), lambda i, j: (i, j, 0)),
        out_shape=jax.ShapeDtypeStruct((b, s, B_W), BF16),
        compiler_params=_cparams("arbitrary", "arbitrary"),
        name="nbr_attention",
    )(q, k, v, bias)


def _sigmoid(z):
    return 1.0 / (1.0 + jnp.exp(-z))


def _mix_body(x_ref, aa_ref, ab_ref, ga_ref, gb_ref, mod_ref, wa_ref, wb_ref, wo_ref,
              gpost_ref, gpre_ref, wr_ref, br_ref, cnt_in_ref,
              x1_ref, h2_ref, route_ref, blk_ref, cnt_ref, run_ref, *, tm):
    first = (pl.program_id(0) == 0) & (pl.program_id(1) == 0)

    @pl.when(first)
    def _():
        run_ref[...] = cnt_in_ref[...]

    ya = jnp.dot(aa_ref[0], wa_ref[...], preferred_element_type=F32)
    yb = jnp.dot(ab_ref[0], wb_ref[...], preferred_element_type=F32)
    merged = _sigmoid(ga_ref[0].astype(F32)) * ya + _sigmoid(gb_ref[0].astype(F32)) * yb
    z = jnp.dot(merged.astype(BF16), wo_ref[...], preferred_element_type=F32)
    gt1 = mod_ref[0, 2:3, :]
    sh2 = mod_ref[0, 3:4, :]
    sc2 = mod_ref[0, 4:5, :]
    x1 = x_ref[0] + gt1 * _rms(z, gpost_ref[...])
    x1_ref[0] = x1
    h2 = _rms(x1, gpre_ref[...]) * (1.0 + sc2) + sh2
    h2_bf = h2.astype(BF16)
    h2_ref[0] = h2_bf

    logits = jnp.dot(h2_bf, wr_ref[...], preferred_element_type=F32) + br_ref[...]
    lane = lax.broadcasted_iota(jnp.int32, (tm, LANES), 1).astype(F32)
    work = logits
    sel = jnp.zeros((tm, LANES), F32)
    vals, idxs = [], []
    for _ in range(TOP_K):
        m = jnp.max(work, axis=-1, keepdims=True)
        idx = jnp.min(jnp.where(work == m, lane, float(LANES)), axis=-1, keepdims=True)
        hit = lane == idx
        vals.append(m)
        idxs.append(idx)
        work = jnp.where(hit, -jnp.inf, work)
        sel = sel + hit.astype(F32)
    es = [jnp.exp(v - vals[0]) for v in vals]
    den = es[0] + es[1] + es[2] + es[3]

    tb = TOKEN_BLOCK
    ri = lax.broadcasted_iota(jnp.int32, (tb, tb), 0)
    ci = lax.broadcasted_iota(jnp.int32, (tb, tb), 1)
    tri = (ci < ri).astype(BF16)
    li = lax.broadcasted_iota(jnp.int32, (LANES, LANES), 0)
    lj = lax.broadcasted_iota(jnp.int32, (LANES, LANES), 1)
    upper = (li < lj).astype(BF16)
    row = lax.broadcasted_iota(jnp.int32, (8, LANES), 0)
    lane_b = lax.broadcasted_iota(jnp.int32, (tb, LANES), 1)
    wts = jnp.zeros((tm, LANES), F32)
    for k in range(TOP_K):
        wts = jnp.where(lane == k, es[k] / den, wts)
    hits = [jnp.where(lane == idxs[k], 1.0, 0.0) for k in range(TOP_K)]
    for sb in range(tm // tb):
        rs = slice(sb * tb, (sb + 1) * tb)
        sel_b = sel[rs]
        earlier = jnp.dot(tri, sel_b.astype(BF16), preferred_element_type=F32)
        n_blk = jnp.sum(sel_b, axis=0, keepdims=True)
        chunks = jnp.floor((n_blk + (ROW_CHUNK - 1.0)) * (1.0 / ROW_CHUNK))
        seg_start = ROW_CHUNK * jnp.dot(jnp.broadcast_to(chunks, (8, LANES)).astype(BF16), upper,
                                        preferred_element_type=F32)[0:1, :]
        slot = earlier + seg_start
        out = wts[rs]
        for k in range(TOP_K):
            local = jnp.sum(hits[k][rs] * slot, axis=-1, keepdims=True)
            out = jnp.where(lane_b == TOP_K + k, local, out)
        route_ref[0, rs, :] = out
        blk_ref[0, sb * 8:(sb + 1) * 8, :] = jnp.where(
            row == 0, n_blk, jnp.where(row == 1, run_ref[...], jnp.where(row == 2, seg_start, 0.0)))
        run_ref[...] = run_ref[...] + ROW_CHUNK * chunks
    cnt_ref[...] = run_ref[...]


def _mix_and_route(x, aa, ab, ga, gb, mod, wa, wb, wo, gpost, gpre, wr, br, cnt_in, tm):
    b, s, d = x.shape
    tok = lambda w: pl.BlockSpec((1, tm, w), lambda i, j: (i, j, 0))
    full = lambda a: pl.BlockSpec(a.shape, lambda i, j: (0,) * a.ndim)
    return pl.pallas_call(
        functools.partial(_mix_body, tm=tm),
        grid=(b, s // tm),
        in_specs=[tok(d), tok(A_Q), tok(B_W), tok(d), tok(d),
                  pl.BlockSpec((1, 6, d), lambda i, j: (i, 0, 0)),
                  full(wa), full(wb), full(wo), full(gpost), full(gpre), full(wr), full(br),
                  full(cnt_in)],
        out_specs=[tok(d), tok(d), tok(LANES),
                   pl.BlockSpec((1, (tm // TOKEN_BLOCK) * 8, LANES), lambda i, j: (i, j, 0)),
                   pl.BlockSpec((1, LANES), lambda i, j: (0, 0))],
        out_shape=[jax.ShapeDtypeStruct((b, s, d), F32), jax.ShapeDtypeStruct((b, s, d), BF16),
                   jax.ShapeDtypeStruct((b, s, LANES), F32),
                   jax.ShapeDtypeStruct((b, (s // TOKEN_BLOCK) * 8, LANES), F32),
                   jax.ShapeDtypeStruct((1, LANES), F32)],
        scratch_shapes=[pltpu.VMEM((1, LANES), F32)],
        compiler_params=_cparams("arbitrary", "arbitrary"),
        name="mix_and_route",
    )(x, aa, ab, ga, gb, mod, wa, wb, wo, gpost, gpre, wr, br, cnt_in)


def _chunk_loop(fn):
    def body(j, carry):
        fn(j)
        return carry

    lax.fori_loop(0, LOCAL_CHUNKS, body, 0, unroll=8)


def _dispatch_body(ctab_ref, tail_ref, rt_ref, hp_ref, hs_ref, xs_ref, loc_ref, zero_ref, sem,
                   *, np_steps, n_steps):
    i = pl.program_id(0)
    slot = i % 2
    d = D_MODEL

    def full_wait(s):
        pltpu.make_async_copy(loc_ref.at[s], xs_ref.at[pl.ds(0, LOCAL_ROWS)], sem.at[s]).wait()

    @pl.when(i == 0)
    def _():
        zero_ref[...] = jnp.zeros_like(zero_ref)

        def chunk_copy(s):
            return pltpu.make_async_copy(zero_ref.at[pl.ds(0, ROW_CHUNK)],
                                         xs_ref.at[pl.ds(s, ROW_CHUNK)], sem.at[0])

        def tails(fn):
            def per_expert(e, carry):
                def chunk(c, carry2):
                    fn(chunk_copy(pl.multiple_of(tail_ref[e] + c * ROW_CHUNK, ROW_CHUNK)))
                    return carry2

                lax.fori_loop(0, tail_ref[N_EXPERTS + e], chunk, 0)
                return carry

            lax.fori_loop(0, N_EXPERTS, per_expert, 0)

        tails(lambda cp: cp.start())
        tails(lambda cp: cp.wait())

        def unused(fn):
            def tile(t, carry):
                r = pl.multiple_of(t * MOE_TILE, MOE_TILE)
                fn(pltpu.make_async_copy(zero_ref, xs_ref.at[pl.ds(r, MOE_TILE)], sem.at[0]))
                return carry

            lax.fori_loop(tail_ref[2 * N_EXPERTS], xs_ref.shape[0] // MOE_TILE, tile, 0)

        unused(lambda cp: cp.start())
        unused(lambda cp: cp.wait())

    @pl.when(i >= 2)
    def _():
        full_wait(slot)

    rt = rt_ref[0]
    srow = lax.broadcasted_iota(jnp.int32, (LOCAL_ROWS, TOKEN_BLOCK), 0).astype(F32)
    hit_any = None
    wsel = jnp.zeros((LOCAL_ROWS, TOKEN_BLOCK), F32)
    for k in range(TOP_K):
        hit = srow == rt[TOP_K + k:TOP_K + k + 1, :]
        wsel = jnp.where(hit, rt[k:k + 1, :], wsel)
        hit_any = hit if hit_any is None else (hit_any | hit)
    onehot = jnp.where(hit_any, 1.0, 0.0).astype(BF16)

    @pl.when(i < np_steps)
    def _():
        loc_ref[slot, :, 0:d] = jnp.dot(onehot, hp_ref[...], preferred_element_type=F32)

    @pl.when(i >= np_steps)
    def _():
        loc_ref[slot, :, 0:d] = jnp.dot(onehot, hs_ref[...], preferred_element_type=F32)

    loc_ref[slot, :, d:d + LANES] = jnp.broadcast_to(jnp.sum(wsel, axis=1, keepdims=True),
                                                     (LOCAL_ROWS, LANES))

    def start(j):
        pltpu.make_async_copy(
            loc_ref.at[slot, pl.ds(pl.multiple_of(j * ROW_CHUNK, ROW_CHUNK), ROW_CHUNK)],
            xs_ref.at[pl.ds(pl.multiple_of(ctab_ref[j], ROW_CHUNK), ROW_CHUNK)], sem.at[slot]).start()

    _chunk_loop(start)

    @pl.when(i == n_steps - 1)
    def _():
        full_wait(slot)
        if n_steps > 1:
            full_wait(1 - slot)


def _dispatch(ctab, tail, rt, h2p, h2s, n_sorted):
    d = h2p.shape[1]
    np_steps = h2p.shape[0] // TOKEN_BLOCK
    ns_steps = h2s.shape[0] // TOKEN_BLOCK
    return pl.pallas_call(
        functools.partial(_dispatch_body, np_steps=np_steps, n_steps=np_steps + ns_steps),
        grid=(np_steps + ns_steps,),
        in_specs=[pl.BlockSpec((CTAB_BLOCK,), lambda i: (i,), memory_space=pltpu.SMEM),
                  pl.BlockSpec((LANES,), lambda i: (0,), memory_space=pltpu.SMEM),
                  pl.BlockSpec((1, 8, TOKEN_BLOCK), lambda i: (i, 0, 0)),
                  pl.BlockSpec((TOKEN_BLOCK, d), lambda i: (jnp.minimum(i, np_steps - 1), 0)),
                  pl.BlockSpec((TOKEN_BLOCK, d),
                               lambda i: (jnp.clip(i - np_steps, 0, ns_steps - 1), 0))],
        out_specs=pl.BlockSpec(memory_space=pl.ANY),
        out_shape=jax.ShapeDtypeStruct((n_sorted, d + LANES), F32),
        scratch_shapes=[pltpu.VMEM((2, LOCAL_ROWS, d + LANES), F32),
                        pltpu.VMEM((MOE_TILE, d + LANES), F32),
                        pltpu.SemaphoreType.DMA((2,))],
        compiler_params=_cparams("arbitrary"),
        name="moe_dispatch",
    )(ctab, tail, rt, h2p, h2s)


def _gmm_body(te_ref, nu_ref, x_ref, wg_ref, bg_ref, wu_ref, bu_ref, wd_ref, bd_ref, y_ref,
              wg_bf, wu_bf, wd_bf):
    i = pl.program_id(0)
    d = D_MODEL
    used = i < nu_ref[0]

    @pl.when(used & ((i == 0) | (te_ref[i] != te_ref[jnp.maximum(i - 1, 0)])))
    def _():
        wg_bf[...] = wg_ref[0].astype(BF16)
        wu_bf[...] = wu_ref[0].astype(BF16)
        wd_bf[...] = wd_ref[0].astype(BF16)

    @pl.when(used)
    def _():
        x = x_ref[:, 0:d].astype(BF16)
        g = jnp.minimum(jnp.dot(x, wg_bf[...], preferred_element_type=F32) + bg_ref[0], SWIGLU_LIMIT)
        u = jnp.clip(jnp.dot(x, wu_bf[...], preferred_element_type=F32) + bu_ref[0],
                     -SWIGLU_LIMIT, SWIGLU_LIMIT)
        act = (u + 1.0) * (g * _sigmoid(SWIGLU_ALPHA * g))
        y = jnp.dot(act.astype(BF16), wd_bf[...], preferred_element_type=F32) + bd_ref[0]
        y_ref[...] = x_ref[:, d:d + 1] * y

    @pl.when(i >= nu_ref[0])
    def _():
        y_ref[...] = jnp.zeros_like(y_ref)


def _grouped_ffn(tile_expert, n_used, xs, wg, bg, wu, bu, wd, bd):
    n = xs.shape[0]
    d, f = wg.shape[1], wg.shape[2]
    wspec = lambda a, b_: pl.BlockSpec((1, a, b_), lambda i, te, nu: (te[i], 0, 0))
    return pl.pallas_call(
        _gmm_body,
        grid_spec=pltpu.PrefetchScalarGridSpec(
            num_scalar_prefetch=2,
            grid=(n // MOE_TILE,),
            in_specs=[pl.BlockSpec((MOE_TILE, d + LANES),
                                   lambda i, te, nu: (jnp.minimum(i, nu[0] - 1), 0)),
                      wspec(d, f), wspec(1, f), wspec(d, f), wspec(1, f), wspec(f, d), wspec(1, d)],
            out_specs=pl.BlockSpec((MOE_TILE, d), lambda i, te, nu: (i, 0)),
            scratch_shapes=[pltpu.VMEM((d, f), BF16), pltpu.VMEM((d, f), BF16),
                            pltpu.VMEM((f, d), BF16)],
        ),
        out_shape=jax.ShapeDtypeStruct((n, d), F32),
        compiler_params=_cparams("arbitrary"),
        name="moe_grouped_ffn",
    )(tile_expert, n_used, xs, wg, bg, wu, bu, wd, bd)


def _combine_body(ctab_ref, cnext_ref, ys_ref, route_ref, x1_ref, mod_ref, g_ref, o_ref, loc_ref, sem,
                  *, n_steps):
    t = pl.program_id(0) * pl.num_programs(1) + pl.program_id(1)
    slot = t % 2

    def fetch(tab_ref, s):
        def start(j):
            pltpu.make_async_copy(
                ys_ref.at[pl.ds(pl.multiple_of(tab_ref[j], ROW_CHUNK), ROW_CHUNK)],
                loc_ref.at[s, pl.ds(pl.multiple_of(j * ROW_CHUNK, ROW_CHUNK), ROW_CHUNK)],
                sem.at[s]).start()

        _chunk_loop(start)

    @pl.when(t == 0)
    def _():
        fetch(ctab_ref, slot)

    @pl.when(t + 1 < n_steps)
    def _():
        fetch(cnext_ref, 1 - slot)

    pltpu.make_async_copy(ys_ref.at[pl.ds(0, LOCAL_ROWS)], loc_ref.at[slot], sem.at[slot]).wait()

    route = route_ref[0]
    col = lax.broadcasted_iota(jnp.int32, (TOKEN_BLOCK, LOCAL_ROWS), 1).astype(F32)
    hit_any = None
    for k in range(TOP_K):
        hit = col == route[:, TOP_K + k:TOP_K + k + 1]
        hit_any = hit if hit_any is None else (hit_any | hit)
    onehot = jnp.where(hit_any, 1.0, 0.0).astype(BF16)
    y2 = jnp.dot(onehot, loc_ref[slot].astype(BF16), preferred_element_type=F32)
    gt2 = mod_ref[0, 5:6, :]
    o_ref[0] = x1_ref[0] + gt2 * _rms(y2, g_ref[...])


def _combine(ctab, first_block, ys, route, x1, mod, g_post):
    b, s, d = x1.shape
    spb = s // TOKEN_BLOCK
    n_steps = b * spb
    tok = lambda w: pl.BlockSpec((1, TOKEN_BLOCK, w), lambda i, j: (i, j, 0))
    blk = lambda i, j: first_block + i * spb + j
    return pl.pallas_call(
        functools.partial(_combine_body, n_steps=n_steps),
        grid=(b, spb),
        in_specs=[pl.BlockSpec((CTAB_BLOCK,), lambda i, j: (blk(i, j),), memory_space=pltpu.SMEM),
                  pl.BlockSpec((CTAB_BLOCK,),
                               lambda i, j: (jnp.minimum(blk(i, j) + 1, first_block + n_steps - 1),),
                               memory_space=pltpu.SMEM),
                  pl.BlockSpec(memory_space=pl.ANY),
                  tok(LANES), tok(d),
                  pl.BlockSpec((1, 6, d), lambda i, j: (i, 0, 0)),
                  pl.BlockSpec((1, d), lambda i, j: (0, 0))],
        out_specs=tok(d),
        out_shape=jax.ShapeDtypeStruct((b, s, d), F32),
        scratch_shapes=[pltpu.VMEM((2, LOCAL_ROWS, d), F32), pltpu.SemaphoreType.DMA((2,))],
        compiler_params=_cparams("arbitrary", "arbitrary"),
        name="moe_combine",
    )(ctab, ctab, ys, route, x1, mod, g_post.reshape(1, d))


def _rope_tables(s_max):
    half = ROT_DIM // 2
    inv = jnp.power(jnp.float32(ROPE_THETA), -jnp.arange(0, ROT_DIM, 2, dtype=F32) / ROT_DIM)
    ang = jnp.arange(s_max, dtype=F32)[:, None] * inv[None, :]
    cos, sin = jnp.cos(ang), jnp.sin(ang)
    d = np.arange(LANES) % HEAD_DIM
    lo = jnp.asarray(d < half)
    hi = jnp.asarray((d >= half) & (d < ROT_DIM))
    cos_l = cos[:, d % half]
    sin_l = sin[:, d % half]
    c = jnp.where(lo | hi, cos_l, 1.0)
    s1 = jnp.where(lo, -sin_l, 0.0)
    s2 = jnp.where(hi, sin_l, 0.0)
    return jnp.stack([c, s1, s2]).astype(F32)


def _route_plan(blk_p, blk_s, counts, n_assign):
    blk = jnp.concatenate([blk_p.reshape(-1, 8, LANES), blk_s.reshape(-1, 8, LANES)], axis=0)
    n_blk = blk[:, 0, :N_EXPERTS].astype(jnp.int32)
    before = blk[:, 1, :N_EXPERTS].astype(jnp.int32)
    cnt = counts[0, :N_EXPERTS].astype(jnp.int32)
    tiles = (cnt + (MOE_TILE - 1)) // MOE_TILE
    tile_end = jnp.cumsum(tiles)
    start = (tile_end - tiles) * MOE_TILE
    n_pad = blk.shape[0] * N_EXPERTS * (ROW_CHUNK - 1)
    n_bound = (n_assign + n_pad) // MOE_TILE + N_EXPERTS
    n_tiles = n_bound + (2 * LOCAL_ROWS + MOE_TILE - 1) // MOE_TILE
    experts = jnp.arange(N_EXPERTS, dtype=jnp.int32)
    chunks = (n_blk + (ROW_CHUNK - 1)) // ROW_CHUNK
    chunk_end = jnp.cumsum(chunks, axis=1)
    j = jnp.arange(LOCAL_CHUNKS, dtype=jnp.int32)
    e = jnp.sum(chunk_end[:, None, :] <= j[None, :, None], axis=-1)
    pick = lambda table: jnp.sum(jnp.where(e[..., None] == experts, table[:, None, :], 0), axis=-1)
    in_segment = pick(start[None, :] + before) + ROW_CHUNK * (j[None, :] - pick(chunk_end - chunks))
    parity = (jnp.arange(blk.shape[0], dtype=jnp.int32) % 2)[:, None]
    scratch_row = n_bound * MOE_TILE + parity * LOCAL_ROWS + ROW_CHUNK * j[None, :]
    ctab = jnp.where(e < N_EXPERTS, in_segment, scratch_row)
    ctab = jnp.pad(ctab, ((0, 0), (0, CTAB_BLOCK - LOCAL_CHUNKS))).reshape(-1).astype(jnp.int32)
    tile_id = jnp.arange(n_tiles, dtype=jnp.int32)
    tile_expert = jnp.minimum(jnp.sum(tile_end[None, :] <= tile_id[:, None], axis=-1),
                              N_EXPERTS - 1).astype(jnp.int32)
    n_used = tile_end[-1:].astype(jnp.int32)
    zero0 = start + cnt
    zero_chunks = (tile_end * MOE_TILE - zero0) // ROW_CHUNK
    tail = jnp.concatenate([zero0, zero_chunks, n_used,
                            jnp.zeros((LANES - 2 * N_EXPERTS - 1,), jnp.int32)])
    return ctab, tail.astype(jnp.int32), tile_expert, n_used, n_tiles * MOE_TILE


def kernel(x_prompt, x_sample, c_prompt, c_sample, ada_w, ada_b, g_pre_mix, g_post_mix, g_pre_ffn,
           g_post_ffn, w_in, sink_a, rpb_b, w_branch_a, w_branch_b, w_out, w_router, b_router,
           w_gate, b_gate, w_up, b_up, w_down, b_down):
    d = D_MODEL
    scale = HEAD_DIM ** -0.5
    g = A_HEADS // A_KV_HEADS
    w = w_in[0]
    col_scale = np.ones((w.shape[1],), np.float32)
    col_scale[:A_Q] = scale
    qb0 = A_Q + 2 * A_KV
    col_scale[qb0:qb0 + B_W] = scale
    wq = w[:, :A_Q].reshape(d, A_KV_HEADS, g, HEAD_DIM).transpose(0, 2, 1, 3).reshape(d, A_Q)
    w_in_bf = (jnp.concatenate([wq, w[:, A_Q:]], axis=1) * col_scale).astype(BF16)
    wa_bf = w_branch_a[0].reshape(A_KV_HEADS, g, HEAD_DIM, d).transpose(1, 0, 2, 3).reshape(
        A_Q, d).astype(BF16)
    wb_bf = w_branch_b[0].astype(BF16)
    wo_bf = w_out[0].astype(BF16)
    wr_bf = jnp.zeros((d, LANES), F32).at[:, :N_EXPERTS].set(w_router[0]).astype(BF16)
    br = jnp.full((1, LANES), NEG, F32).at[0, :N_EXPERTS].set(b_router[0])
    wg_bf, wu_bf, wd_bf = w_gate[0], w_up[0], w_down[0]
    bg, bu, bd = (t[0].reshape(N_EXPERTS, 1, -1) for t in (b_gate, b_up, b_down))
    sink_rows = jnp.repeat(sink_a[0].reshape(A_KV_HEADS, g).T, A_BLOCK, axis=0).astype(F32)

    nb_p = x_prompt.shape[0]
    mod = _modulation(jnp.concatenate([c_prompt, c_sample], axis=0), ada_w[0], ada_b[0])
    mod = mod.reshape(-1, 6, d)
    s_max = max(x_prompt.shape[1], x_sample.shape[1])
    rope = _rope_tables(s_max)
    bias = _nbr_bias(rpb_b[0])

    def front(x, m, cnt_in):
        qa, ka, va, qb, kb, vb, ga, gb = _in_projection(x, m, g_pre_mix[0], w_in_bf, rope, 512)
        aa = _window_attention(qa, ka, va, sink_rows, 512)
        ab = _nbr_attention(qb, kb, vb, bias, 512)
        return _mix_and_route(x, aa, ab, ga, gb, m, wa_bf, wb_bf, wo_bf,
                              g_post_mix[0].reshape(1, d), g_pre_ffn[0].reshape(1, d),
                              wr_bf, br, cnt_in, 2 * TOKEN_BLOCK)

    mod_p, mod_s = mod[:nb_p], mod[nb_p:]
    x1p, h2p, route_p, blk_p, cnt_p = front(x_prompt, mod_p, jnp.zeros((1, LANES), F32))
    x1s, h2s, route_s, blk_s, cnt = front(x_sample, mod_s, cnt_p)

    h2p, h2s = h2p.reshape(-1, d), h2s.reshape(-1, d)
    n_assign = (h2p.shape[0] + h2s.shape[0]) * TOP_K
    ctab, tail, tile_expert, n_used, n_sorted = _route_plan(blk_p, blk_s, cnt, n_assign)

    def by_block(route):
        return route[..., :2 * TOP_K].reshape(-1, TOKEN_BLOCK, 2 * TOP_K).transpose(0, 2, 1)

    rt = jnp.concatenate([by_block(route_p), by_block(route_s)], axis=0)
    xs = _dispatch(ctab, tail, rt, h2p, h2s, n_sorted)
    ys = _grouped_ffn(tile_expert, n_used, xs, wg_bf, bg, wu_bf, bu, wd_bf, bd)
    y_p = _combine(ctab, 0, ys, route_p, x1p, mod_p, g_post_ffn[0])
    y_s = _combine(ctab, h2p.shape[0] // TOKEN_BLOCK, ys, route_s, x1s, mod_s, g_post_ffn[0])
    return (y_p, y_s)
```

```python
import functools

import jax
import jax.numpy as jnp
import numpy as np
from jax import lax
from jax.experimental import pallas as pl
from jax.experimental.pallas import tpu as pltpu

D_MODEL = 1024
HEAD_DIM = 64
ROT_DIM = HEAD_DIM // 4
ROPE_THETA = 500000.0
A_HEADS = 8
A_KV_HEADS = 2
WINDOW = 128
A_BLOCK = 128
B_HEADS = 8
GRID_W = 64
NA_ROWS = 8
NA_COLS = 16
N_EXPERTS = 32
TOP_K = 4
SWIGLU_LIMIT = 7.0
SWIGLU_ALPHA = 1.702
RMS_EPS = 1e-6
NEG = -1e30
A_Q = A_HEADS * HEAD_DIM
A_KV = A_KV_HEADS * HEAD_DIM
B_W = B_HEADS * HEAD_DIM

LANES = 128
PAIR = 2 * HEAD_DIM
N_PAIRS = B_W // PAIR
MOE_TILE = 512
TOKEN_BLOCK = 256
ROW_CHUNK_LOG2 = 3
ROW_CHUNK = 1 << ROW_CHUNK_LOG2
LOCAL_ROWS = TOKEN_BLOCK * TOP_K + N_EXPERTS * ROW_CHUNK
LOCAL_CHUNKS = LOCAL_ROWS // ROW_CHUNK
CTAB_BLOCK = 2 * LANES
VMEM_LIMIT = 56 * 1024 * 1024

BF16 = jnp.bfloat16
F32 = jnp.float32


def _cparams(*sem):
    return pltpu.CompilerParams(dimension_semantics=sem, vmem_limit_bytes=VMEM_LIMIT)


def _mod_body(c_ref, w_ref, b_ref, o_ref):
    c = c_ref[...]
    s = c * (1.0 / (1.0 + jnp.exp(-c)))
    o_ref[...] = jnp.dot(s, w_ref[...], preferred_element_type=F32,
                         precision=lax.Precision.HIGHEST) + b_ref[...]


def _modulation(c, ada_w, ada_b):
    n, d = c.shape
    blk = 1024
    return pl.pallas_call(
        _mod_body,
        grid=(ada_w.shape[1] // blk,),
        in_specs=[pl.BlockSpec((n, d), lambda j: (0, 0)),
                  pl.BlockSpec((d, blk), lambda j: (0, j)),
                  pl.BlockSpec((1, blk), lambda j: (0, j))],
        out_specs=pl.BlockSpec((n, blk), lambda j: (0, j)),
        out_shape=jax.ShapeDtypeStruct((n, ada_w.shape[1]), F32),
        compiler_params=_cparams("arbitrary"),
        name="adaln_mod",
    )(c, ada_w, ada_b.reshape(1, -1))


def _rms(x, g):
    return x * lax.rsqrt(jnp.mean(x * x, axis=-1, keepdims=True) + RMS_EPS) * g


def _rope_tile(seg, rope_ref):
    return (seg * rope_ref[0] + pltpu.roll(seg, LANES - ROT_DIM // 2, 1) * rope_ref[1]
            + pltpu.roll(seg, ROT_DIM // 2, 1) * rope_ref[2])


def _inproj_body(x_ref, mod_ref, g_ref, w_ref, rope_ref,
                 qa_ref, ka_ref, va_ref, qb_ref, kb_ref, vb_ref, ga_ref, gb_ref):
    sh1 = mod_ref[0, 0:1, :]
    sc1 = mod_ref[0, 1:2, :]
    h = (_rms(x_ref[0], g_ref[...]) * (1.0 + sc1) + sh1).astype(BF16)

    def proj(lo, hi):
        return jnp.dot(h, w_ref[:, lo:hi], preferred_element_type=F32)

    qa = proj(0, A_Q)
    for t in range(A_Q // LANES):
        qa_ref[0, :, t * LANES:(t + 1) * LANES] = _rope_tile(
            qa[:, t * LANES:(t + 1) * LANES], rope_ref).astype(BF16)
    o = A_Q
    kva = proj(o, o + 2 * A_KV)
    ka_ref[0] = _rope_tile(kva[:, :A_KV], rope_ref).astype(BF16)
    va_ref[0] = kva[:, A_KV:].astype(BF16)
    o += 2 * A_KV
    for ref in (qb_ref, kb_ref, vb_ref):
        ref[0] = proj(o, o + B_W).astype(BF16)
        o += B_W
    for ref in (ga_ref, gb_ref):
        ref[0] = proj(o, o + D_MODEL).astype(BF16)
        o += D_MODEL


def _in_projection(x, mod, g_pre, w_in_bf, rope, tm):
    b, s, d = x.shape
    widths = (A_Q, A_KV, A_KV, B_W, B_W, B_W, D_MODEL, D_MODEL)
    tok = lambda w: pl.BlockSpec((1, tm, w), lambda i, j: (i, j, 0))
    return pl.pallas_call(
        _inproj_body,
        grid=(b, s // tm),
        in_specs=[tok(d),
                  pl.BlockSpec((1, 6, d), lambda i, j: (i, 0, 0)),
                  pl.BlockSpec((1, d), lambda i, j: (0, 0)),
                  pl.BlockSpec(w_in_bf.shape, lambda i, j: (0, 0)),
                  pl.BlockSpec((3, tm, LANES), lambda i, j: (0, j, 0))],
        out_specs=[tok(w) for w in widths],
        out_shape=[jax.ShapeDtypeStruct((b, s, w), BF16) for w in widths],
        compiler_params=_cparams("arbitrary", "arbitrary"),
        name="in_projection",
    )(x, mod, g_pre.reshape(1, d), w_in_bf, rope)


def _stack_heads(t):
    lane = lax.broadcasted_iota(jnp.int32, t.shape, 1)
    zero = jnp.zeros_like(t)
    return jnp.concatenate([jnp.where(lane < HEAD_DIM, t, zero),
                            jnp.where(lane >= HEAD_DIM, t, zero)], axis=0)


def _window_body(q_ref, k_ref, v_ref, sink_ref, band_ref, o_ref, *, seq, tq):
    span = A_BLOCK + 2 * WINDOW
    lane_o = lax.broadcasted_iota(jnp.int32, (A_BLOCK, LANES), 1)
    for sub in range(tq // A_BLOCK):
        q0 = pl.program_id(1) * tq + sub * A_BLOCK
        k0 = pl.multiple_of(jnp.clip(q0 - WINDOW, 0, seq - span), A_BLOCK)
        kw = k_ref[0, pl.ds(k0, span), :]
        vw = v_ref[0, pl.ds(k0, span), :]
        band = band_ref[lax.div(q0 - k0, A_BLOCK)]
        for pr in range(N_PAIRS):
            qs = _stack_heads(
                q_ref[0, sub * A_BLOCK:(sub + 1) * A_BLOCK, pr * LANES:(pr + 1) * LANES])
            st = lax.dot_general(kw, qs, (((1,), (1,)), ((), ())), preferred_element_type=F32) + band
            snk = sink_ref[pr]
            m = jnp.maximum(jnp.max(st, axis=0, keepdims=True), snk)
            e = jnp.exp(st - m)
            inv = 1.0 / (jnp.sum(e, axis=0, keepdims=True) + jnp.exp(snk - m))
            o = lax.dot_general((e * inv).astype(BF16), vw, (((0,), (0,)), ((), ())),
                                preferred_element_type=F32)
            o_ref[0, sub * A_BLOCK:(sub + 1) * A_BLOCK, pr * LANES:(pr + 1) * LANES] = jnp.where(
                lane_o < HEAD_DIM, o[:A_BLOCK], o[A_BLOCK:]).astype(BF16)


def _window_attention(q, k, v, sink_rows, tq):
    b, s, _ = q.shape
    span = A_BLOCK + 2 * WINDOW
    off = np.arange(3)[:, None, None] * A_BLOCK
    kj = np.arange(span)[None, :, None]
    qi = (np.arange(A_KV_HEADS * A_BLOCK) % A_BLOCK)[None, None, :]
    band = jnp.asarray(np.where(np.abs(qi + off - kj) <= WINDOW, 0.0, NEG).astype(np.float32))
    return pl.pallas_call(
        functools.partial(_window_body, seq=s, tq=tq),
        grid=(b, s // tq),
        in_specs=[pl.BlockSpec((1, tq, A_Q), lambda i, j: (i, j, 0)),
                  pl.BlockSpec((1, s, A_KV), lambda i, j: (i, 0, 0)),
                  pl.BlockSpec((1, s, A_KV), lambda i, j: (i, 0, 0)),
                  pl.BlockSpec(sink_rows.shape, lambda i, j: (0, 0, 0)),
                  pl.BlockSpec(band.shape, lambda i, j: (0, 0, 0))],
        out_specs=pl.BlockSpec((1, tq, A_Q), lambda i, j: (i, j, 0)),
        out_shape=jax.ShapeDtypeStruct((b, s, A_Q), BF16),
        compiler_params=_cparams("arbitrary", "arbitrary"),
        name="window_attention",
    )(q, k, v, sink_rows, band)


def _nbr_bias_body(rpb_ref, o_ref):
    p = pl.program_id(0)
    d = pl.program_id(1)
    nkeys = NA_ROWS * GRID_W
    row = lax.broadcasted_iota(jnp.int32, (2 * GRID_W, nkeys), 0)
    col = lax.broadcasted_iota(jnp.int32, (2 * GRID_W, nkeys), 1)
    qc = row % GRID_W
    kc = col % GRID_W
    qstart = jnp.clip(qc - NA_COLS // 2, 0, GRID_W - NA_COLS)
    valid = (kc >= qstart) & (kc < qstart + NA_COLS)
    cidx = kc - qc + NA_COLS - 1
    key_row = lax.broadcasted_iota(jnp.int32, (1, nkeys), 1) // GRID_W
    acc = jnp.full((2 * GRID_W, nkeys), NEG, F32)
    for j in range(2 * NA_COLS - 1):
        rvs = []
        for hh in range(2):
            rv = jnp.zeros((1, nkeys), F32)
            for aa in range(NA_ROWS):
                rv = jnp.where(key_row == aa, rpb_ref[2 * p + hh, aa - d + NA_ROWS - 1, j], rv)
            rvs.append(rv)
        acc = jnp.where(valid & (cidx == j), jnp.where(row < GRID_W, rvs[0], rvs[1]), acc)
    o_ref[0, 0] = acc.T


def _nbr_bias(rpb):
    nkeys = NA_ROWS * GRID_W
    return pl.pallas_call(
        _nbr_bias_body,
        grid=(N_PAIRS, NA_ROWS),
        in_specs=[pl.BlockSpec(memory_space=pltpu.SMEM)],
        out_specs=pl.BlockSpec((1, 1, nkeys, 2 * GRID_W), lambda p, d: (p, d, 0, 0)),
        out_shape=jax.ShapeDtypeStruct((N_PAIRS, NA_ROWS, nkeys, 2 * GRID_W), F32),
        compiler_params=_cparams("arbitrary", "arbitrary"),
        name="nbr_bias",
    )(rpb)


def _nbr_body(q_ref, k_ref, v_ref, bias_ref, o_ref, *, seq, tq):
    n_rows = seq // GRID_W
    nkeys = NA_ROWS * GRID_W
    lane_o = lax.broadcasted_iota(jnp.int32, (GRID_W, LANES), 1)

    def row(rl, carry):
        r = pl.program_id(1) * (tq // GRID_W) + rl
        r0 = jnp.clip(r - NA_ROWS // 2, 0, n_rows - NA_ROWS)
        d = r - r0
        k0 = pl.multiple_of(r0 * GRID_W, GRID_W)
        q0 = pl.multiple_of(rl * GRID_W, GRID_W)
        for pr in range(N_PAIRS):
            lanes = slice(pr * LANES, (pr + 1) * LANES)
            kw = k_ref[0, pl.ds(k0, nkeys), lanes]
            vw = v_ref[0, pl.ds(k0, nkeys), lanes]
            qs = _stack_heads(q_ref[0, pl.ds(q0, GRID_W), lanes])
            st = lax.dot_general(kw, qs, (((1,), (1,)), ((), ())), preferred_element_type=F32)
            st = st + bias_ref[pr, d]
            m = jnp.max(st, axis=0, keepdims=True)
            e = jnp.exp(st - m)
            inv = 1.0 / jnp.sum(e, axis=0, keepdims=True)
            o = lax.dot_general((e * inv).astype(BF16), vw, (((0,), (0,)), ((), ())),
                                preferred_element_type=F32)
            o_ref[0, pl.ds(q0, GRID_W), lanes] = jnp.where(
                lane_o < HEAD_DIM, o[:GRID_W], o[GRID_W:]).astype(BF16)
        return carry

    lax.fori_loop(0, tq // GRID_W, row, 0, unroll=2)


def _nbr_attention(q, k, v, bias, tq):
    b, s, _ = q.shape
    return pl.pallas_call(
        functools.partial(_nbr_body, seq=s, tq=tq),
        grid=(b, s // tq),
        in_specs=[pl.BlockSpec((1, tq, B_W), lambda i, j: (i, j, 0)),
                  pl.BlockSpec((1, s, B_W), lambda i, j: (i, 0, 0)),
                  pl.BlockSpec((1, s, B_W), lambda i, j: (i, 0, 0)),
                  pl.BlockSpec(bias.shape, lambda i, j: (0, 0, 0, 0))],
        out_specs=pl.BlockSpec((1, tq, B_W), lambda i, j: (i, j, 0)),
        out_shape=jax.ShapeDtypeStruct((b, s, B_W), BF16),
        compiler_params=_cparams("arbitrary", "arbitrary"),
        name="nbr_attention",
    )(q, k, v, bias)


def _sigmoid(z):
    return 1.0 / (1.0 + jnp.exp(-z))


def _mix_body(x_ref, aa_ref, ab_ref, ga_ref, gb_ref, mod_ref, wa_ref, wb_ref, wo_ref,
              gpost_ref, gpre_ref, wr_ref, br_ref, cnt_in_ref,
              x1_ref, h2_ref, route_ref, blk_ref, cnt_ref, run_ref, *, tm):
    first = (pl.program_id(0) == 0) & (pl.program_id(1) == 0)

    @pl.when(first)
    def _():
        run_ref[...] = cnt_in_ref[...]

    ya = jnp.dot(aa_ref[0], wa_ref[...], preferred_element_type=F32)
    yb = jnp.dot(ab_ref[0], wb_ref[...], preferred_element_type=F32)
    merged = _sigmoid(ga_ref[0].astype(F32)) * ya + _sigmoid(gb_ref[0].astype(F32)) * yb
    z = jnp.dot(merged.astype(BF16), wo_ref[...], preferred_element_type=F32)
    gt1 = mod_ref[0, 2:3, :]
    sh2 = mod_ref[0, 3:4, :]
    sc2 = mod_ref[0, 4:5, :]
    x1 = x_ref[0] + gt1 * _rms(z, gpost_ref[...])
    x1_ref[0] = x1
    h2 = _rms(x1, gpre_ref[...]) * (1.0 + sc2) + sh2
    h2_bf = h2.astype(BF16)
    h2_ref[0] = h2_bf

    logits = jnp.dot(h2_bf, wr_ref[...], preferred_element_type=F32) + br_ref[...]
    lane = lax.broadcasted_iota(jnp.int32, (tm, LANES), 1).astype(F32)
    work = logits
    sel = jnp.zeros((tm, LANES), F32)
    vals, idxs = [], []
    for _ in range(TOP_K):
        m = jnp.max(work, axis=-1, keepdims=True)
        idx = jnp.min(jnp.where(work == m, lane, float(LANES)), axis=-1, keepdims=True)
        hit = lane == idx
        vals.append(m)
        idxs.append(idx)
        work = jnp.where(hit, -jnp.inf, work)
        sel = sel + hit.astype(F32)
    es = [jnp.exp(v - vals[0]) for v in vals]
    den = es[0] + es[1] + es[2] + es[3]

    tb = TOKEN_BLOCK
    ri = lax.broadcasted_iota(jnp.int32, (tb, tb), 0)
    ci = lax.broadcasted_iota(jnp.int32, (tb, tb), 1)
    tri = (ci < ri).astype(BF16)
    li = lax.broadcasted_iota(jnp.int32, (LANES, LANES), 0)
    lj = lax.broadcasted_iota(jnp.int32, (LANES, LANES), 1)
    upper = (li < lj).astype(BF16)
    row = lax.broadcasted_iota(jnp.int32, (8, LANES), 0)
    lane_b = lax.broadcasted_iota(jnp.int32, (tb, LANES), 1)
    wts = jnp.zeros((tm, LANES), F32)
    for k in range(TOP_K):
        wts = jnp.where(lane == k, es[k] / den, wts)
    hits = [jnp.where(lane == idxs[k], 1.0, 0.0) for k in range(TOP_K)]
    for sb in range(tm // tb):
        rs = slice(sb * tb, (sb + 1) * tb)
        sel_b = sel[rs]
        earlier = jnp.dot(tri, sel_b.astype(BF16), preferred_element_type=F32)
        n_blk = jnp.sum(sel_b, axis=0, keepdims=True)
        chunks = jnp.floor((n_blk + (ROW_CHUNK - 1.0)) * (1.0 / ROW_CHUNK))
        seg_start = ROW_CHUNK * jnp.dot(jnp.broadcast_to(chunks, (8, LANES)).astype(BF16), upper,
                                        preferred_element_type=F32)[0:1, :]
        slot = earlier + seg_start
        out = wts[rs]
        for k in range(TOP_K):
            local = jnp.sum(hits[k][rs] * slot, axis=-1, keepdims=True)
            out = jnp.where(lane_b == TOP_K + k, local, out)
        route_ref[0, rs, :] = out
        blk_ref[0, sb * 8:(sb + 1) * 8, :] = jnp.where(
            row == 0, n_blk, jnp.where(row == 1, run_ref[...], jnp.where(row == 2, seg_start, 0.0)))
        run_ref[...] = run_ref[...] + ROW_CHUNK * chunks
    cnt_ref[...] = run_ref[...]


def _mix_and_route(x, aa, ab, ga, gb, mod, wa, wb, wo, gpost, gpre, wr, br, cnt_in, tm):
    b, s, d = x.shape
    tok = lambda w: pl.BlockSpec((1, tm, w), lambda i, j: (i, j, 0))
    full = lambda a: pl.BlockSpec(a.shape, lambda i, j: (0,) * a.ndim)
    return pl.pallas_call(
        functools.partial(_mix_body, tm=tm),
        grid=(b, s // tm),
        in_specs=[tok(d), tok(A_Q), tok(B_W), tok(d), tok(d),
                  pl.BlockSpec((1, 6, d), lambda i, j: (i, 0, 0)),
                  full(wa), full(wb), full(wo), full(gpost), full(gpre), full(wr), full(br),
                  full(cnt_in)],
        out_specs=[tok(d), tok(d), tok(LANES),
                   pl.BlockSpec((1, (tm // TOKEN_BLOCK) * 8, LANES), lambda i, j: (i, j, 0)),
                   pl.BlockSpec((1, LANES), lambda i, j: (0, 0))],
        out_shape=[jax.ShapeDtypeStruct((b, s, d), F32), jax.ShapeDtypeStruct((b, s, d), BF16),
                   jax.ShapeDtypeStruct((b, s, LANES), F32),
                   jax.ShapeDtypeStruct((b, (s // TOKEN_BLOCK) * 8, LANES), F32),
                   jax.ShapeDtypeStruct((1, LANES), F32)],
        scratch_shapes=[pltpu.VMEM((1, LANES), F32)],
        compiler_params=_cparams("arbitrary", "arbitrary"),
        name="mix_and_route",
    )(x, aa, ab, ga, gb, mod, wa, wb, wo, gpost, gpre, wr, br, cnt_in)


def _chunk_loop(fn):
    def body(j, carry):
        fn(j)
        return carry

    lax.fori_loop(0, LOCAL_CHUNKS, body, 0, unroll=8)


def _dispatch_body(ctab_ref, tail_ref, rt_ref, hp_ref, hs_ref, xs_ref, loc_ref, zero_ref, sem,
                   *, np_steps, n_steps):
    i = pl.program_id(0)
    slot = i % 2
    d = D_MODEL

    def full_wait(s):
        pltpu.make_async_copy(loc_ref.at[s], xs_ref.at[pl.ds(0, LOCAL_ROWS)], sem.at[s]).wait()

    @pl.when(i == 0)
    def _():
        zero_ref[...] = jnp.zeros_like(zero_ref)

        def chunk_copy(s):
            return pltpu.make_async_copy(zero_ref.at[pl.ds(0, ROW_CHUNK)],
                                         xs_ref.at[pl.ds(s, ROW_CHUNK)], sem.at[0])

        def tails(fn):
            def per_expert(e, carry):
                def chunk(c, carry2):
                    fn(chunk_copy(pl.multiple_of(tail_ref[e] + c * ROW_CHUNK, ROW_CHUNK)))
                    return carry2

                lax.fori_loop(0, tail_ref[N_EXPERTS + e], chunk, 0)
                return carry

            lax.fori_loop(0, N_EXPERTS, per_expert, 0)

        tails(lambda cp: cp.start())
        tails(lambda cp: cp.wait())

        def unused(fn):
            def tile(t, carry):
                r = pl.multiple_of(t * MOE_TILE, MOE_TILE)
                fn(pltpu.make_async_copy(zero_ref, xs_ref.at[pl.ds(r, MOE_TILE)], sem.at[0]))
                return carry

            lax.fori_loop(tail_ref[2 * N_EXPERTS], xs_ref.shape[0] // MOE_TILE, tile, 0)

        unused(lambda cp: cp.start())
        unused(lambda cp: cp.wait())

    @pl.when(i >= 2)
    def _():
        full_wait(slot)

    rt = rt_ref[0]
    srow = lax.broadcasted_iota(jnp.int32, (LOCAL_ROWS, TOKEN_BLOCK), 0).astype(F32)
    hit_any = None
    wsel = jnp.zeros((LOCAL_ROWS, TOKEN_BLOCK), F32)
    for k in range(TOP_K):
        hit = srow == rt[TOP_K + k:TOP_K + k + 1, :]
        wsel = jnp.where(hit, rt[k:k + 1, :], wsel)
        hit_any = hit if hit_any is None else (hit_any | hit)
    onehot = jnp.where(hit_any, 1.0, 0.0).astype(BF16)

    @pl.when(i < np_steps)
    def _():
        loc_ref[slot, :, 0:d] = jnp.dot(onehot, hp_ref[...], preferred_element_type=F32)

    @pl.when(i >= np_steps)
    def _():
        loc_ref[slot, :, 0:d] = jnp.dot(onehot, hs_ref[...], preferred_element_type=F32)

    loc_ref[slot, :, d:d + LANES] = jnp.broadcast_to(jnp.sum(wsel, axis=1, keepdims=True),
                                                     (LOCAL_ROWS, LANES))

    def start(j):
        pltpu.make_async_copy(
            loc_ref.at[slot, pl.ds(pl.multiple_of(j * ROW_CHUNK, ROW_CHUNK), ROW_CHUNK)],
            xs_ref.at[pl.ds(pl.multiple_of(ctab_ref[j], ROW_CHUNK), ROW_CHUNK)], sem.at[slot]).start()

    _chunk_loop(start)

    @pl.when(i == n_steps - 1)
    def _():
        full_wait(slot)
        if n_steps > 1:
            full_wait(1 - slot)


def _dispatch(ctab, tail, rt, h2p, h2s, n_sorted):
    d = h2p.shape[1]
    np_steps = h2p.shape[0] // TOKEN_BLOCK
    ns_steps = h2s.shape[0] // TOKEN_BLOCK
    return pl.pallas_call(
        functools.partial(_dispatch_body, np_steps=np_steps, n_steps=np_steps + ns_steps),
        grid=(np_steps + ns_steps,),
        in_specs=[pl.BlockSpec((CTAB_BLOCK,), lambda i: (i,), memory_space=pltpu.SMEM),
                  pl.BlockSpec((LANES,), lambda i: (0,), memory_space=pltpu.SMEM),
                  pl.BlockSpec((1, 8, TOKEN_BLOCK), lambda i: (i, 0, 0)),
                  pl.BlockSpec((TOKEN_BLOCK, d), lambda i: (jnp.minimum(i, np_steps - 1), 0)),
                  pl.BlockSpec((TOKEN_BLOCK, d),
                               lambda i: (jnp.clip(i - np_steps, 0, ns_steps - 1), 0))],
        out_specs=pl.BlockSpec(memory_space=pl.ANY),
        out_shape=jax.ShapeDtypeStruct((n_sorted, d + LANES), F32),
        scratch_shapes=[pltpu.VMEM((2, LOCAL_ROWS, d + LANES), F32),
                        pltpu.VMEM((MOE_TILE, d + LANES), F32),
                        pltpu.SemaphoreType.DMA((2,))],
        compiler_params=_cparams("arbitrary"),
        name="moe_dispatch",
    )(ctab, tail, rt, h2p, h2s)


def _gmm_body(te_ref, nu_ref, x_ref, wg_ref, bg_ref, wu_ref, bu_ref, wd_ref, bd_ref, y_ref,
              wg_bf, wu_bf, wd_bf):
    i = pl.program_id(0)
    d = D_MODEL
    used = i < nu_ref[0]

    @pl.when(used & ((i == 0) | (te_ref[i] != te_ref[jnp.maximum(i - 1, 0)])))
    def _():
        wg_bf[...] = wg_ref[0].astype(BF16)
        wu_bf[...] = wu_ref[0].astype(BF16)
        wd_bf[...] = wd_ref[0].astype(BF16)

    @pl.when(used)
    def _():
        x = x_ref[:, 0:d].astype(BF16)
        g = jnp.minimum(jnp.dot(x, wg_bf[...], preferred_element_type=F32) + bg_ref[0], SWIGLU_LIMIT)
        u = jnp.clip(jnp.dot(x, wu_bf[...], preferred_element_type=F32) + bu_ref[0],
                     -SWIGLU_LIMIT, SWIGLU_LIMIT)
        act = (u + 1.0) * (g * _sigmoid(SWIGLU_ALPHA * g))
        y = jnp.dot(act.astype(BF16), wd_bf[...], preferred_element_type=F32) + bd_ref[0]
        y_ref[...] = x_ref[:, d:d + 1] * y

    @pl.when(i >= nu_ref[0])
    def _():
        y_ref[...] = jnp.zeros_like(y_ref)


def _grouped_ffn(tile_expert, n_used, xs, wg, bg, wu, bu, wd, bd):
    n = xs.shape[0]
    d, f = wg.shape[1], wg.shape[2]
    wspec = lambda a, b_: pl.BlockSpec((1, a, b_), lambda i, te, nu: (te[i], 0, 0))
    return pl.pallas_call(
        _gmm_body,
        grid_spec=pltpu.PrefetchScalarGridSpec(
            num_scalar_prefetch=2,
            grid=(n // MOE_TILE,),
            in_specs=[pl.BlockSpec((MOE_TILE, d + LANES),
                                   lambda i, te, nu: (jnp.minimum(i, nu[0] - 1), 0)),
                      wspec(d, f), wspec(1, f), wspec(d, f), wspec(1, f), wspec(f, d), wspec(1, d)],
            out_specs=pl.BlockSpec((MOE_TILE, d), lambda i, te, nu: (i, 0)),
            scratch_shapes=[pltpu.VMEM((d, f), BF16), pltpu.VMEM((d, f), BF16),
                            pltpu.VMEM((f, d), BF16)],
        ),
        out_shape=jax.ShapeDtypeStruct((n, d), F32),
        compiler_params=_cparams("arbitrary"),
        name="moe_grouped_ffn",
    )(tile_expert, n_used, xs, wg, bg, wu, bu, wd, bd)


def _combine_body(ctab_ref, cnext_ref, ys_ref, route_ref, x1_ref, mod_ref, g_ref, o_ref, loc_ref, sem,
                  *, n_steps):
    t = pl.program_id(0) * pl.num_programs(1) + pl.program_id(1)
    slot = t % 2

    def fetch(tab_ref, s):
        def start(j):
            pltpu.make_async_copy(
                ys_ref.at[pl.ds(pl.multiple_of(tab_ref[j], ROW_CHUNK), ROW_CHUNK)],
                loc_ref.at[s, pl.ds(pl.multiple_of(j * ROW_CHUNK, ROW_CHUNK), ROW_CHUNK)],
                sem.at[s]).start()

        _chunk_loop(start)

    @pl.when(t == 0)
    def _():
        fetch(ctab_ref, slot)

    @pl.when(t + 1 < n_steps)
    def _():
        fetch(cnext_ref, 1 - slot)

    pltpu.make_async_copy(ys_ref.at[pl.ds(0, LOCAL_ROWS)], loc_ref.at[slot], sem.at[slot]).wait()

    route = route_ref[0]
    col = lax.broadcasted_iota(jnp.int32, (TOKEN_BLOCK, LOCAL_ROWS), 1).astype(F32)
    hit_any = None
    for k in range(TOP_K):
        hit = col == route[:, TOP_K + k:TOP_K + k + 1]
        hit_any = hit if hit_any is None else (hit_any | hit)
    onehot = jnp.where(hit_any, 1.0, 0.0).astype(BF16)
    y2 = jnp.dot(onehot, loc_ref[slot].astype(BF16), preferred_element_type=F32)
    gt2 = mod_ref[0, 5:6, :]
    o_ref[0] = x1_ref[0] + gt2 * _rms(y2, g_ref[...])


def _combine(ctab, first_block, ys, route, x1, mod, g_post):
    b, s, d = x1.shape
    spb = s // TOKEN_BLOCK
    n_steps = b * spb
    tok = lambda w: pl.BlockSpec((1, TOKEN_BLOCK, w), lambda i, j: (i, j, 0))
    blk = lambda i, j: first_block + i * spb + j
    return pl.pallas_call(
        functools.partial(_combine_body, n_steps=n_steps),
        grid=(b, spb),
        in_specs=[pl.BlockSpec((CTAB_BLOCK,), lambda i, j: (blk(i, j),), memory_space=pltpu.SMEM),
                  pl.BlockSpec((CTAB_BLOCK,),
                               lambda i, j: (jnp.minimum(blk(i, j) + 1, first_block + n_steps - 1),),
                               memory_space=pltpu.SMEM),
                  pl.BlockSpec(memory_space=pl.ANY),
                  tok(LANES), tok(d),
                  pl.BlockSpec((1, 6, d), lambda i, j: (i, 0, 0)),
                  pl.BlockSpec((1, d), lambda i, j: (0, 0))],
        out_specs=tok(d),
        out_shape=jax.ShapeDtypeStruct((b, s, d), F32),
        scratch_shapes=[pltpu.VMEM((2, LOCAL_ROWS, d), F32), pltpu.SemaphoreType.DMA((2,))],
        compiler_params=_cparams("arbitrary", "arbitrary"),
        name="moe_combine",
    )(ctab, ctab, ys, route, x1, mod, g_post.reshape(1, d))


def _rope_tables(s_max):
    half = ROT_DIM // 2
    inv = jnp.power(jnp.float32(ROPE_THETA), -jnp.arange(0, ROT_DIM, 2, dtype=F32) / ROT_DIM)
    ang = jnp.arange(s_max, dtype=F32)[:, None] * inv[None, :]
    cos, sin = jnp.cos(ang), jnp.sin(ang)
    d = np.arange(LANES) % HEAD_DIM
    lo = jnp.asarray(d < half)
    hi = jnp.asarray((d >= half) & (d < ROT_DIM))
    cos_l = cos[:, d % half]
    sin_l = sin[:, d % half]
    c = jnp.where(lo | hi, cos_l, 1.0)
    s1 = jnp.where(lo, -sin_l, 0.0)
    s2 = jnp.where(hi, sin_l, 0.0)
    return jnp.stack([c, s1, s2]).astype(F32)


def _route_plan(blk_p, blk_s, counts, n_assign):
    blk = jnp.concatenate([blk_p.reshape(-1, 8, LANES), blk_s.reshape(-1, 8, LANES)], axis=0)
    n_blk = blk[:, 0, :N_EXPERTS].astype(jnp.int32)
    before = blk[:, 1, :N_EXPERTS].astype(jnp.int32)
    cnt = counts[0, :N_EXPERTS].astype(jnp.int32)
    tiles = (cnt + (MOE_TILE - 1)) // MOE_TILE
    tile_end = jnp.cumsum(tiles)
    start = (tile_end - tiles) * MOE_TILE
    n_pad = blk.shape[0] * N_EXPERTS * (ROW_CHUNK - 1)
    n_bound = (n_assign + n_pad) // MOE_TILE + N_EXPERTS
    n_tiles = n_bound + (2 * LOCAL_ROWS + MOE_TILE - 1) // MOE_TILE
    experts = jnp.arange(N_EXPERTS, dtype=jnp.int32)
    chunks = (n_blk + (ROW_CHUNK - 1)) // ROW_CHUNK
    chunk_end = jnp.cumsum(chunks, axis=1)
    j = jnp.arange(LOCAL_CHUNKS, dtype=jnp.int32)
    e = jnp.sum(chunk_end[:, None, :] <= j[None, :, None], axis=-1)
    pick = lambda table: jnp.sum(jnp.where(e[..., None] == experts, table[:, None, :], 0), axis=-1)
    in_segment = pick(start[None, :] + before) + ROW_CHUNK * (j[None, :] - pick(chunk_end - chunks))
    parity = (jnp.arange(blk.shape[0], dtype=jnp.int32) % 2)[:, None]
    scratch_row = n_bound * MOE_TILE + parity * LOCAL_ROWS + ROW_CHUNK * j[None, :]
    ctab = jnp.where(e < N_EXPERTS, in_segment, scratch_row)
    ctab = jnp.pad(ctab, ((0, 0), (0, CTAB_BLOCK - LOCAL_CHUNKS))).reshape(-1).astype(jnp.int32)
    tile_id = jnp.arange(n_tiles, dtype=jnp.int32)
    tile_expert = jnp.minimum(jnp.sum(tile_end[None, :] <= tile_id[:, None], axis=-1),
                              N_EXPERTS - 1).astype(jnp.int32)
    n_used = tile_end[-1:].astype(jnp.int32)
    zero0 = start + cnt
    zero_chunks = (tile_end * MOE_TILE - zero0) // ROW_CHUNK
    tail = jnp.concatenate([zero0, zero_chunks, n_used,
                            jnp.zeros((LANES - 2 * N_EXPERTS - 1,), jnp.int32)])
    return ctab, tail.astype(jnp.int32), tile_expert, n_used, n_tiles * MOE_TILE


def kernel(x_prompt, x_sample, c_prompt, c_sample, ada_w, ada_b, g_pre_mix, g_post_mix, g_pre_ffn,
           g_post_ffn, w_in, sink_a, rpb_b, w_branch_a, w_branch_b, w_out, w_router, b_router,
           w_gate, b_gate, w_up, b_up, w_down, b_down):
    d = D_MODEL
    scale = HEAD_DIM ** -0.5
    g = A_HEADS // A_KV_HEADS
    w = w_in[0]
    col_scale = np.ones((w.shape[1],), np.float32)
    col_scale[:A_Q] = scale
    qb0 = A_Q + 2 * A_KV
    col_scale[qb0:qb0 + B_W] = scale
    wq = w[:, :A_Q].reshape(d, A_KV_HEADS, g, HEAD_DIM).transpose(0, 2, 1, 3).reshape(d, A_Q)
    w_in_bf = (jnp.concatenate([wq, w[:, A_Q:]], axis=1) * col_scale).astype(BF16)
    wa_bf = w_branch_a[0].reshape(A_KV_HEADS, g, HEAD_DIM, d).transpose(1, 0, 2, 3).reshape(
        A_Q, d).astype(BF16)
    wb_bf = w_branch_b[0].astype(BF16)
    wo_bf = w_out[0].astype(BF16)
    wr_bf = jnp.zeros((d, LANES), F32).at[:, :N_EXPERTS].set(w_router[0]).astype(BF16)
    br = jnp.full((1, LANES), NEG, F32).at[0, :N_EXPERTS].set(b_router[0])
    wg_bf, wu_bf, wd_bf = w_gate[0], w_up[0], w_down[0]
    bg, bu, bd = (t[0].reshape(N_EXPERTS, 1, -1) for t in (b_gate, b_up, b_down))
    sink_rows = jnp.repeat(sink_a[0].astype(F32).reshape(A_KV_HEADS, g).T, A_BLOCK,
                           axis=1).reshape(g, 1, A_KV_HEADS * A_BLOCK)

    nb_p = x_prompt.shape[0]
    mod = _modulation(jnp.concatenate([c_prompt, c_sample], axis=0), ada_w[0], ada_b[0])
    mod = mod.reshape(-1, 6, d)
    s_max = max(x_prompt.shape[1], x_sample.shape[1])
    rope = _rope_tables(s_max)
    bias = _nbr_bias(rpb_b[0])

    def front(x, m, cnt_in):
        qa, ka, va, qb, kb, vb, ga, gb = _in_projection(x, m, g_pre_mix[0], w_in_bf, rope, 512)
        aa = _window_attention(qa, ka, va, sink_rows, 512)
        ab = _nbr_attention(qb, kb, vb, bias, 512)
        return _mix_and_route(x, aa, ab, ga, gb, m, wa_bf, wb_bf, wo_bf,
                              g_post_mix[0].reshape(1, d), g_pre_ffn[0].reshape(1, d),
                              wr_bf, br, cnt_in, 2 * TOKEN_BLOCK)

    mod_p, mod_s = mod[:nb_p], mod[nb_p:]
    x1p, h2p, route_p, blk_p, cnt_p = front(x_prompt, mod_p, jnp.zeros((1, LANES), F32))
    x1s, h2s, route_s, blk_s, cnt = front(x_sample, mod_s, cnt_p)

    h2p, h2s = h2p.reshape(-1, d), h2s.reshape(-1, d)
    n_assign = (h2p.shape[0] + h2s.shape[0]) * TOP_K
    ctab, tail, tile_expert, n_used, n_sorted = _route_plan(blk_p, blk_s, cnt, n_assign)

    def by_block(route):
        return route[..., :2 * TOP_K].reshape(-1, TOKEN_BLOCK, 2 * TOP_K).transpose(0, 2, 1)

    rt = jnp.concatenate([by_block(route_p), by_block(route_s)], axis=0)
    xs = _dispatch(ctab, tail, rt, h2p, h2s, n_sorted)
    ys = _grouped_ffn(tile_expert, n_used, xs, wg_bf, bg, wu_bf, bu, wd_bf, bd)
    y_p = _combine(ctab, 0, ys, route_p, x1p, mod_p, g_post_ffn[0])
    y_s = _combine(ctab, h2p.shape[0] // TOKEN_BLOCK, ys, route_s, x1s, mod_s, g_post_ffn[0])
    return (y_p, y_s)
```

```python
import functools

import jax
import jax.numpy as jnp
import numpy as np
from jax import lax
from jax.experimental import pallas as pl
from jax.experimental.pallas import tpu as pltpu

D_MODEL = 1024
HEAD_DIM = 64
ROT_DIM = HEAD_DIM // 4
ROPE_THETA = 500000.0
A_HEADS = 8
A_KV_HEADS = 2
WINDOW = 128
A_BLOCK = 128
B_HEADS = 8
GRID_W = 64
NA_ROWS = 8
NA_COLS = 16
N_EXPERTS = 32
TOP_K = 4
SWIGLU_LIMIT = 7.0
SWIGLU_ALPHA = 1.702
RMS_EPS = 1e-6
NEG = -1e30
A_Q = A_HEADS * HEAD_DIM
A_KV = A_KV_HEADS * HEAD_DIM
B_W = B_HEADS * HEAD_DIM

LANES = 128
PAIR = 2 * HEAD_DIM
N_PAIRS = B_W // PAIR
MOE_TILE = 512
TOKEN_BLOCK = 256
ROW_CHUNK_LOG2 = 3
ROW_CHUNK = 1 << ROW_CHUNK_LOG2
LOCAL_ROWS = TOKEN_BLOCK * TOP_K + N_EXPERTS * ROW_CHUNK
LOCAL_CHUNKS = LOCAL_ROWS // ROW_CHUNK
CTAB_BLOCK = 2 * LANES
DISPATCH_GROUP = 2
IN_FLIGHT_BLOCKS = 2 * DISPATCH_GROUP
VMEM_LIMIT = 56 * 1024 * 1024

BF16 = jnp.bfloat16
F32 = jnp.float32


def _cparams(*sem):
    return pltpu.CompilerParams(dimension_semantics=sem, vmem_limit_bytes=VMEM_LIMIT)


def _mod_body(c_ref, w_ref, b_ref, o_ref):
    c = c_ref[...]
    s = c * (1.0 / (1.0 + jnp.exp(-c)))
    o_ref[...] = jnp.dot(s, w_ref[...], preferred_element_type=F32,
                         precision=lax.Precision.HIGHEST) + b_ref[...]


def _modulation(c, ada_w, ada_b):
    n, d = c.shape
    blk = 1024
    return pl.pallas_call(
        _mod_body,
        grid=(ada_w.shape[1] // blk,),
        in_specs=[pl.BlockSpec((n, d), lambda j: (0, 0)),
                  pl.BlockSpec((d, blk), lambda j: (0, j)),
                  pl.BlockSpec((1, blk), lambda j: (0, j))],
        out_specs=pl.BlockSpec((n, blk), lambda j: (0, j)),
        out_shape=jax.ShapeDtypeStruct((n, ada_w.shape[1]), F32),
        compiler_params=_cparams("arbitrary"),
        name="adaln_mod",
    )(c, ada_w, ada_b.reshape(1, -1))


def _rms(x, g):
    return x * lax.rsqrt(jnp.mean(x * x, axis=-1, keepdims=True) + RMS_EPS) * g


def _rope_tile(seg, rope_ref):
    return (seg * rope_ref[0] + pltpu.roll(seg, LANES - ROT_DIM // 2, 1) * rope_ref[1]
            + pltpu.roll(seg, ROT_DIM // 2, 1) * rope_ref[2])


def _inproj_body(x_ref, mod_ref, g_ref, w_ref, rope_ref,
                 qa_ref, ka_ref, va_ref, qb_ref, kb_ref, vb_ref, ga_ref, gb_ref):
    sh1 = mod_ref[0, 0:1, :]
    sc1 = mod_ref[0, 1:2, :]
    h = (_rms(x_ref[0], g_ref[...]) * (1.0 + sc1) + sh1).astype(BF16)

    def proj(lo, hi):
        return jnp.dot(h, w_ref[:, lo:hi], preferred_element_type=F32)

    qa = proj(0, A_Q)
    for t in range(A_Q // LANES):
        qa_ref[0, :, t * LANES:(t + 1) * LANES] = _rope_tile(
            qa[:, t * LANES:(t + 1) * LANES], rope_ref).astype(BF16)
    o = A_Q
    kva = proj(o, o + 2 * A_KV)
    ka_ref[0] = _rope_tile(kva[:, :A_KV], rope_ref).astype(BF16)
    va_ref[0] = kva[:, A_KV:].astype(BF16)
    o += 2 * A_KV
    for ref in (qb_ref, kb_ref, vb_ref):
        ref[0] = proj(o, o + B_W).astype(BF16)
        o += B_W
    for ref in (ga_ref, gb_ref):
        ref[0] = proj(o, o + D_MODEL).astype(BF16)
        o += D_MODEL


def _in_projection(x, mod, g_pre, w_in_bf, rope, tm):
    b, s, d = x.shape
    widths = (A_Q, A_KV, A_KV, B_W, B_W, B_W, D_MODEL, D_MODEL)
    tok = lambda w: pl.BlockSpec((1, tm, w), lambda i, j: (i, j, 0))
    return pl.pallas_call(
        _inproj_body,
        grid=(b, s // tm),
        in_specs=[tok(d),
                  pl.BlockSpec((1, 6, d), lambda i, j: (i, 0, 0)),
                  pl.BlockSpec((1, d), lambda i, j: (0, 0)),
                  pl.BlockSpec(w_in_bf.shape, lambda i, j: (0, 0)),
                  pl.BlockSpec((3, tm, LANES), lambda i, j: (0, j, 0))],
        out_specs=[tok(w) for w in widths],
        out_shape=[jax.ShapeDtypeStruct((b, s, w), BF16) for w in widths],
        compiler_params=_cparams("arbitrary", "arbitrary"),
        name="in_projection",
    )(x, mod, g_pre.reshape(1, d), w_in_bf, rope)


def _stack_heads(t):
    lane = lax.broadcasted_iota(jnp.int32, t.shape, 1)
    zero = jnp.zeros_like(t)
    return jnp.concatenate([jnp.where(lane < HEAD_DIM, t, zero),
                            jnp.where(lane >= HEAD_DIM, t, zero)], axis=0)


def _window_body(q_ref, k_ref, v_ref, sink_ref, band_ref, o_ref, *, seq, tq):
    span = A_BLOCK + 2 * WINDOW
    lane_o = lax.broadcasted_iota(jnp.int32, (A_BLOCK, LANES), 1)
    for sub in range(tq // A_BLOCK):
        q0 = pl.program_id(1) * tq + sub * A_BLOCK
        k0 = pl.multiple_of(jnp.clip(q0 - WINDOW, 0, seq - span), A_BLOCK)
        kw = k_ref[0, pl.ds(k0, span), :]
        vw = v_ref[0, pl.ds(k0, span), :]
        band = band_ref[lax.div(q0 - k0, A_BLOCK)]
        for pr in range(N_PAIRS):
            qs = _stack_heads(
                q_ref[0, sub * A_BLOCK:(sub + 1) * A_BLOCK, pr * LANES:(pr + 1) * LANES])
            st = lax.dot_general(kw, qs, (((1,), (1,)), ((), ())), preferred_element_type=F32) + band
            snk = sink_ref[pr]
            m = jnp.maximum(jnp.max(st, axis=0, keepdims=True), snk)
            e = jnp.exp(st - m)
            inv = 1.0 / (jnp.sum(e, axis=0, keepdims=True) + jnp.exp(snk - m))
            o = lax.dot_general((e * inv).astype(BF16), vw, (((0,), (0,)), ((), ())),
                                preferred_element_type=F32)
            o_ref[0, sub * A_BLOCK:(sub + 1) * A_BLOCK, pr * LANES:(pr + 1) * LANES] = jnp.where(
                lane_o < HEAD_DIM, o[:A_BLOCK], o[A_BLOCK:]).astype(BF16)


def _window_attention(q, k, v, sink_rows, tq):
    b, s, _ = q.shape
    span = A_BLOCK + 2 * WINDOW
    off = np.arange(3)[:, None, None] * A_BLOCK
    kj = np.arange(span)[None, :, None]
    qi = (np.arange(A_KV_HEADS * A_BLOCK) % A_BLOCK)[None, None, :]
    band = jnp.asarray(np.where(np.abs(qi + off - kj) <= WINDOW, 0.0, NEG).astype(np.float32))
    return pl.pallas_call(
        functools.partial(_window_body, seq=s, tq=tq),
        grid=(b, s // tq),
        in_specs=[pl.BlockSpec((1, tq, A_Q), lambda i, j: (i, j, 0)),
                  pl.BlockSpec((1, s, A_KV), lambda i, j: (i, 0, 0)),
                  pl.BlockSpec((1, s, A_KV), lambda i, j: (i, 0, 0)),
                  pl.BlockSpec(sink_rows.shape, lambda i, j: (0, 0, 0)),
                  pl.BlockSpec(band.shape, lambda i, j: (0, 0, 0))],
        out_specs=pl.BlockSpec((1, tq, A_Q), lambda i, j: (i, j, 0)),
        out_shape=jax.ShapeDtypeStruct((b, s, A_Q), BF16),
        compiler_params=_cparams("arbitrary", "arbitrary"),
        name="window_attention",
    )(q, k, v, sink_rows, band)


def _nbr_bias_body(rpb_ref, o_ref):
    p = pl.program_id(0)
    d = pl.program_id(1)
    nkeys = NA_ROWS * GRID_W
    row = lax.broadcasted_iota(jnp.int32, (2 * GRID_W, nkeys), 0)
    col = lax.broadcasted_iota(jnp.int32, (2 * GRID_W, nkeys), 1)
    qc = row % GRID_W
    kc = col % GRID_W
    qstart = jnp.clip(qc - NA_COLS // 2, 0, GRID_W - NA_COLS)
    valid = (kc >= qstart) & (kc < qstart + NA_COLS)
    cidx = kc - qc + NA_COLS - 1
    key_row = lax.broadcasted_iota(jnp.int32, (1, nkeys), 1) // GRID_W
    acc = jnp.full((2 * GRID_W, nkeys), NEG, F32)
    for j in range(2 * NA_COLS - 1):
        rvs = []
        for hh in range(2):
            rv = jnp.zeros((1, nkeys), F32)
            for aa in range(NA_ROWS):
                rv = jnp.where(key_row == aa, rpb_ref[2 * p + hh, aa - d + NA_ROWS - 1, j], rv)
            rvs.append(rv)
        acc = jnp.where(valid & (cidx == j), jnp.where(row < GRID_W, rvs[0], rvs[1]), acc)
    o_ref[0, 0] = acc.T


def _nbr_bias(rpb):
    nkeys = NA_ROWS * GRID_W
    return pl.pallas_call(
        _nbr_bias_body,
        grid=(N_PAIRS, NA_ROWS),
        in_specs=[pl.BlockSpec(memory_space=pltpu.SMEM)],
        out_specs=pl.BlockSpec((1, 1, nkeys, 2 * GRID_W), lambda p, d: (p, d, 0, 0)),
        out_shape=jax.ShapeDtypeStruct((N_PAIRS, NA_ROWS, nkeys, 2 * GRID_W), F32),
        compiler_params=_cparams("arbitrary", "arbitrary"),
        name="nbr_bias",
    )(rpb)


def _nbr_body(q_ref, k_ref, v_ref, bias_ref, o_ref, *, seq, tq):
    n_rows = seq // GRID_W
    nkeys = NA_ROWS * GRID_W
    lane_o = lax.broadcasted_iota(jnp.int32, (GRID_W, LANES), 1)

    for rl in range(tq // GRID_W):
        r = pl.program_id(1) * (tq // GRID_W) + rl
        r0 = jnp.clip(r - NA_ROWS // 2, 0, n_rows - NA_ROWS)
        d = r - r0
        k0 = pl.multiple_of(r0 * GRID_W, GRID_W)
        q0 = rl * GRID_W
        for pr in range(N_PAIRS):
            lanes = slice(pr * LANES, (pr + 1) * LANES)
            kw = k_ref[0, pl.ds(k0, nkeys), lanes]
            vw = v_ref[0, pl.ds(k0, nkeys), lanes]
            qs = _stack_heads(q_ref[0, pl.ds(q0, GRID_W), lanes])
            st = lax.dot_general(kw, qs, (((1,), (1,)), ((), ())), preferred_element_type=F32)
            st = st + bias_ref[pr, d]
            m = jnp.max(st, axis=0, keepdims=True)
            e = jnp.exp(st - m)
            inv = 1.0 / jnp.sum(e, axis=0, keepdims=True)
            o = lax.dot_general((e * inv).astype(BF16), vw, (((0,), (0,)), ((), ())),
                                preferred_element_type=F32)
            o_ref[0, pl.ds(q0, GRID_W), lanes] = jnp.where(
                lane_o < HEAD_DIM, o[:GRID_W], o[GRID_W:]).astype(BF16)


def _nbr_attention(q, k, v, bias, tq):
    b, s, _ = q.shape
    return pl.pallas_call(
        functools.partial(_nbr_body, seq=s, tq=tq),
        grid=(b, s // tq),
        in_specs=[pl.BlockSpec((1, tq, B_W), lambda i, j: (i, j, 0)),
                  pl.BlockSpec((1, s, B_W), lambda i, j: (i, 0, 0)),
                  pl.BlockSpec((1, s, B_W), lambda i, j: (i, 0, 0)),
                  pl.BlockSpec(bias.shape, lambda i, j: (0, 0, 0, 0))],
        out_specs=pl.BlockSpec((1, tq, B_W), lambda i, j: (i, j, 0)),
        out_shape=jax.ShapeDtypeStruct((b, s, B_W), BF16),
        compiler_params=_cparams("arbitrary", "arbitrary"),
        name="nbr_attention",
    )(q, k, v, bias)


def _sigmoid(z):
    return 1.0 / (1.0 + jnp.exp(-z))


def _mix_body(x_ref, aa_ref, ab_ref, ga_ref, gb_ref, mod_ref, wa_ref, wb_ref, wo_ref,
              gpost_ref, gpre_ref, wr_ref, br_ref, cnt_in_ref,
              x1_ref, h2_ref, route_ref, blk_ref, cnt_ref, run_ref, *, tm):
    first = (pl.program_id(0) == 0) & (pl.program_id(1) == 0)

    @pl.when(first)
    def _():
        run_ref[...] = cnt_in_ref[...]

    ya = jnp.dot(aa_ref[0], wa_ref[...], preferred_element_type=F32)
    yb = jnp.dot(ab_ref[0], wb_ref[...], preferred_element_type=F32)
    merged = _sigmoid(ga_ref[0].astype(F32)) * ya + _sigmoid(gb_ref[0].astype(F32)) * yb
    z = jnp.dot(merged.astype(BF16), wo_ref[...], preferred_element_type=F32)
    gt1 = mod_ref[0, 2:3, :]
    sh2 = mod_ref[0, 3:4, :]
    sc2 = mod_ref[0, 4:5, :]
    x1 = x_ref[0] + gt1 * _rms(z, gpost_ref[...])
    x1_ref[0] = x1
    h2 = _rms(x1, gpre_ref[...]) * (1.0 + sc2) + sh2
    h2_bf = h2.astype(BF16)
    h2_ref[0] = h2_bf

    logits = jnp.dot(h2_bf, wr_ref[...], preferred_element_type=F32) + br_ref[...]
    lane = lax.broadcasted_iota(jnp.int32, (tm, LANES), 1).astype(F32)
    work = logits
    sel = jnp.zeros((tm, LANES), F32)
    vals, idxs = [], []
    for _ in range(TOP_K):
        m = jnp.max(work, axis=-1, keepdims=True)
        idx = jnp.min(jnp.where(work == m, lane, float(LANES)), axis=-1, keepdims=True)
        hit = lane == idx
        vals.append(m)
        idxs.append(idx)
        work = jnp.where(hit, -jnp.inf, work)
        sel = sel + hit.astype(F32)
    es = [jnp.exp(v - vals[0]) for v in vals]
    den = es[0] + es[1] + es[2] + es[3]

    tb = TOKEN_BLOCK
    ri = lax.broadcasted_iota(jnp.int32, (tb, tb), 0)
    ci = lax.broadcasted_iota(jnp.int32, (tb, tb), 1)
    tri = (ci < ri).astype(BF16)
    li = lax.broadcasted_iota(jnp.int32, (LANES, LANES), 0)
    lj = lax.broadcasted_iota(jnp.int32, (LANES, LANES), 1)
    upper = (li < lj).astype(BF16)
    row = lax.broadcasted_iota(jnp.int32, (8, LANES), 0)
    lane_b = lax.broadcasted_iota(jnp.int32, (tb, LANES), 1)
    wts = jnp.zeros((tm, LANES), F32)
    for k in range(TOP_K):
        wts = jnp.where(lane == k, es[k] / den, wts)
    hits = [jnp.where(lane == idxs[k], 1.0, 0.0) for k in range(TOP_K)]
    for sb in range(tm // tb):
        rs = slice(sb * tb, (sb + 1) * tb)
        sel_b = sel[rs]
        earlier = jnp.dot(tri, sel_b.astype(BF16), preferred_element_type=F32)
        n_blk = jnp.sum(sel_b, axis=0, keepdims=True)
        chunks = jnp.floor((n_blk + (ROW_CHUNK - 1.0)) * (1.0 / ROW_CHUNK))
        seg_start = ROW_CHUNK * jnp.dot(jnp.broadcast_to(chunks, (8, LANES)).astype(BF16), upper,
                                        preferred_element_type=F32)[0:1, :]
        slot = earlier + seg_start
        out = wts[rs]
        for k in range(TOP_K):
            local = jnp.sum(hits[k][rs] * slot, axis=-1, keepdims=True)
            out = jnp.where(lane_b == TOP_K + k, local, out)
        route_ref[0, rs, :] = out
        blk_ref[0, sb * 8:(sb + 1) * 8, :] = jnp.where(
            row == 0, n_blk, jnp.where(row == 1, run_ref[...], jnp.where(row == 2, seg_start, 0.0)))
        run_ref[...] = run_ref[...] + ROW_CHUNK * chunks
    cnt_ref[...] = run_ref[...]


def _mix_and_route(x, aa, ab, ga, gb, mod, wa, wb, wo, gpost, gpre, wr, br, cnt_in, tm):
    b, s, d = x.shape
    tok = lambda w: pl.BlockSpec((1, tm, w), lambda i, j: (i, j, 0))
    full = lambda a: pl.BlockSpec(a.shape, lambda i, j: (0,) * a.ndim)
    return pl.pallas_call(
        functools.partial(_mix_body, tm=tm),
        grid=(b, s // tm),
        in_specs=[tok(d), tok(A_Q), tok(B_W), tok(d), tok(d),
                  pl.BlockSpec((1, 6, d), lambda i, j: (i, 0, 0)),
                  full(wa), full(wb), full(wo), full(gpost), full(gpre), full(wr), full(br),
                  full(cnt_in)],
        out_specs=[tok(d), tok(d), tok(LANES),
                   pl.BlockSpec((1, (tm // TOKEN_BLOCK) * 8, LANES), lambda i, j: (i, j, 0)),
                   pl.BlockSpec((1, LANES), lambda i, j: (0, 0))],
        out_shape=[jax.ShapeDtypeStruct((b, s, d), F32), jax.ShapeDtypeStruct((b, s, d), BF16),
                   jax.ShapeDtypeStruct((b, s, LANES), F32),
                   jax.ShapeDtypeStruct((b, (s // TOKEN_BLOCK) * 8, LANES), F32),
                   jax.ShapeDtypeStruct((1, LANES), F32)],
        scratch_shapes=[pltpu.VMEM((1, LANES), F32)],
        compiler_params=_cparams("arbitrary", "arbitrary"),
        name="mix_and_route",
    )(x, aa, ab, ga, gb, mod, wa, wb, wo, gpost, gpre, wr, br, cnt_in)


def _chunk_loop(fn):
    def body(j, carry):
        fn(j)
        return carry

    lax.fori_loop(0, LOCAL_CHUNKS, body, 0, unroll=8)


def _dispatch_body(ctab_ref, tail_ref, rt_ref, hp_ref, hs_ref, xs_ref, loc_ref, zero_ref, sem,
                   *, np_steps, n_steps):
    i = pl.program_id(0)
    slot = i % 2
    d = D_MODEL
    grp = DISPATCH_GROUP

    def full_wait(s):
        for b in range(grp):
            pltpu.make_async_copy(loc_ref.at[s * grp + b], xs_ref.at[pl.ds(0, LOCAL_ROWS)],
                                  sem.at[s]).wait()

    @pl.when(i == 0)
    def _():
        zero_ref[...] = jnp.zeros_like(zero_ref)

        def chunk_copy(s):
            return pltpu.make_async_copy(zero_ref.at[pl.ds(0, ROW_CHUNK)],
                                         xs_ref.at[pl.ds(s, ROW_CHUNK)], sem.at[0])

        def tails(fn):
            def per_expert(e, carry):
                def chunk(c, carry2):
                    fn(chunk_copy(pl.multiple_of(tail_ref[e] + c * ROW_CHUNK, ROW_CHUNK)))
                    return carry2

                lax.fori_loop(0, tail_ref[N_EXPERTS + e], chunk, 0)
                return carry

            lax.fori_loop(0, N_EXPERTS, per_expert, 0)

        tails(lambda cp: cp.start())
        tails(lambda cp: cp.wait())

        def unused(fn):
            def tile(t, carry):
                r = pl.multiple_of(t * MOE_TILE, MOE_TILE)
                fn(pltpu.make_async_copy(zero_ref, xs_ref.at[pl.ds(r, MOE_TILE)], sem.at[0]))
                return carry

            lax.fori_loop(tail_ref[2 * N_EXPERTS], xs_ref.shape[0] // MOE_TILE, tile, 0)

        unused(lambda cp: cp.start())
        unused(lambda cp: cp.wait())

    @pl.when(i >= 2)
    def _():
        full_wait(slot)

    srow = lax.broadcasted_iota(jnp.int32, (LOCAL_ROWS, TOKEN_BLOCK), 0).astype(F32)
    for b in range(grp):
        buf = slot * grp + b
        rows = slice(b * TOKEN_BLOCK, (b + 1) * TOKEN_BLOCK)
        rt = rt_ref[b]
        hit_any = None
        wsel = jnp.zeros((LOCAL_ROWS, TOKEN_BLOCK), F32)
        for k in range(TOP_K):
            hit = srow == rt[TOP_K + k:TOP_K + k + 1, :]
            wsel = jnp.where(hit, rt[k:k + 1, :], wsel)
            hit_any = hit if hit_any is None else (hit_any | hit)
        onehot = jnp.where(hit_any, 1.0, 0.0).astype(BF16)
        h = jnp.where(i < np_steps, hp_ref[rows, :], hs_ref[rows, :])
        loc_ref[buf, :, 0:d] = jnp.dot(onehot, h, preferred_element_type=F32)
        loc_ref[buf, :, d:d + LANES] = jnp.broadcast_to(jnp.sum(wsel, axis=1, keepdims=True),
                                                        (LOCAL_ROWS, LANES))

    for b in range(grp):
        def start(j, b=b):
            pltpu.make_async_copy(
                loc_ref.at[slot * grp + b, pl.ds(pl.multiple_of(j * ROW_CHUNK, ROW_CHUNK), ROW_CHUNK)],
                xs_ref.at[pl.ds(pl.multiple_of(ctab_ref[b * CTAB_BLOCK + j], ROW_CHUNK), ROW_CHUNK)],
                sem.at[slot]).start()

        _chunk_loop(start)

    @pl.when(i == n_steps - 1)
    def _():
        full_wait(slot)
        if n_steps > 1:
            full_wait(1 - slot)


def _dispatch(ctab, tail, rt, h2p, h2s, n_sorted):
    d = h2p.shape[1]
    grp = DISPATCH_GROUP
    rows = grp * TOKEN_BLOCK
    np_steps = h2p.shape[0] // rows
    ns_steps = h2s.shape[0] // rows
    return pl.pallas_call(
        functools.partial(_dispatch_body, np_steps=np_steps, n_steps=np_steps + ns_steps),
        grid=(np_steps + ns_steps,),
        in_specs=[pl.BlockSpec((grp * CTAB_BLOCK,), lambda i: (i,), memory_space=pltpu.SMEM),
                  pl.BlockSpec((LANES,), lambda i: (0,), memory_space=pltpu.SMEM),
                  pl.BlockSpec((grp, 8, TOKEN_BLOCK), lambda i: (i, 0, 0)),
                  pl.BlockSpec((rows, d), lambda i: (jnp.minimum(i, np_steps - 1), 0)),
                  pl.BlockSpec((rows, d), lambda i: (jnp.clip(i - np_steps, 0, ns_steps - 1), 0))],
        out_specs=pl.BlockSpec(memory_space=pl.ANY),
        out_shape=jax.ShapeDtypeStruct((n_sorted, d + LANES), F32),
        scratch_shapes=[pltpu.VMEM((2 * grp, LOCAL_ROWS, d + LANES), F32),
                        pltpu.VMEM((MOE_TILE, d + LANES), F32),
                        pltpu.SemaphoreType.DMA((2,))],
        compiler_params=_cparams("arbitrary"),
        name="moe_dispatch",
    )(ctab, tail, rt, h2p, h2s)


def _gmm_body(te_ref, nu_ref, x_ref, wg_ref, bg_ref, wu_ref, bu_ref, wd_ref, bd_ref, y_ref,
              wg_bf, wu_bf, wd_bf):
    i = pl.program_id(0)
    d = D_MODEL
    used = i < nu_ref[0]

    @pl.when(used & ((i == 0) | (te_ref[i] != te_ref[jnp.maximum(i - 1, 0)])))
    def _():
        wg_bf[...] = wg_ref[0].astype(BF16)
        wu_bf[...] = wu_ref[0].astype(BF16)
        wd_bf[...] = wd_ref[0].astype(BF16)

    @pl.when(used)
    def _():
        x = x_ref[:, 0:d].astype(BF16)
        g = jnp.minimum(jnp.dot(x, wg_bf[...], preferred_element_type=F32) + bg_ref[0], SWIGLU_LIMIT)
        u = jnp.clip(jnp.dot(x, wu_bf[...], preferred_element_type=F32) + bu_ref[0],
                     -SWIGLU_LIMIT, SWIGLU_LIMIT)
        act = (u + 1.0) * (g * _sigmoid(SWIGLU_ALPHA * g))
        y = jnp.dot(act.astype(BF16), wd_bf[...], preferred_element_type=F32) + bd_ref[0]
        y_ref[...] = x_ref[:, d:d + 1] * y

    @pl.when(i >= nu_ref[0])
    def _():
        y_ref[...] = jnp.zeros_like(y_ref)


def _grouped_ffn(tile_expert, n_used, xs, wg, bg, wu, bu, wd, bd):
    n = xs.shape[0]
    d, f = wg.shape[1], wg.shape[2]
    wspec = lambda a, b_: pl.BlockSpec((1, a, b_), lambda i, te, nu: (te[i], 0, 0))
    return pl.pallas_call(
        _gmm_body,
        grid_spec=pltpu.PrefetchScalarGridSpec(
            num_scalar_prefetch=2,
            grid=(n // MOE_TILE,),
            in_specs=[pl.BlockSpec((MOE_TILE, d + LANES),
                                   lambda i, te, nu: (jnp.minimum(i, nu[0] - 1), 0)),
                      wspec(d, f), wspec(1, f), wspec(d, f), wspec(1, f), wspec(f, d), wspec(1, d)],
            out_specs=pl.BlockSpec((MOE_TILE, d), lambda i, te, nu: (i, 0)),
            scratch_shapes=[pltpu.VMEM((d, f), BF16), pltpu.VMEM((d, f), BF16),
                            pltpu.VMEM((f, d), BF16)],
        ),
        out_shape=jax.ShapeDtypeStruct((n, d), F32),
        compiler_params=_cparams("arbitrary"),
        name="moe_grouped_ffn",
    )(tile_expert, n_used, xs, wg, bg, wu, bu, wd, bd)


def _combine_body(ctab_ref, cnext_ref, ys_ref, route_ref, x1_ref, mod_ref, g_ref, o_ref, loc_ref, sem,
                  *, n_steps):
    t = pl.program_id(0) * pl.num_programs(1) + pl.program_id(1)
    slot = t % 2
    grp = DISPATCH_GROUP

    def fetch(tab_ref, s):
        for b in range(grp):
            def start(j, b=b):
                pltpu.make_async_copy(
                    ys_ref.at[pl.ds(pl.multiple_of(tab_ref[b * CTAB_BLOCK + j], ROW_CHUNK), ROW_CHUNK)],
                    loc_ref.at[s * grp + b, pl.ds(pl.multiple_of(j * ROW_CHUNK, ROW_CHUNK), ROW_CHUNK)],
                    sem.at[s]).start()

            _chunk_loop(start)

    @pl.when(t == 0)
    def _():
        fetch(ctab_ref, slot)

    @pl.when(t + 1 < n_steps)
    def _():
        fetch(cnext_ref, 1 - slot)

    for b in range(grp):
        pltpu.make_async_copy(ys_ref.at[pl.ds(0, LOCAL_ROWS)], loc_ref.at[slot * grp + b],
                              sem.at[slot]).wait()

    gt2 = mod_ref[0, 5:6, :]
    col = lax.broadcasted_iota(jnp.int32, (TOKEN_BLOCK, LOCAL_ROWS), 1).astype(F32)
    for b in range(grp):
        rows = slice(b * TOKEN_BLOCK, (b + 1) * TOKEN_BLOCK)
        route = route_ref[0, rows, :]
        hit_any = None
        for k in range(TOP_K):
            hit = col == route[:, TOP_K + k:TOP_K + k + 1]
            hit_any = hit if hit_any is None else (hit_any | hit)
        onehot = jnp.where(hit_any, 1.0, 0.0).astype(BF16)
        y2 = jnp.dot(onehot, loc_ref[slot * grp + b].astype(BF16), preferred_element_type=F32)
        o_ref[0, rows, :] = x1_ref[0, rows, :] + gt2 * _rms(y2, g_ref[...])


def _combine(ctab, first_block, ys, route, x1, mod, g_post):
    b, s, d = x1.shape
    grp = DISPATCH_GROUP
    rows = grp * TOKEN_BLOCK
    spb = s // rows
    n_steps = b * spb
    first = first_block // grp
    tok = lambda w: pl.BlockSpec((1, rows, w), lambda i, j: (i, j, 0))
    step = lambda i, j: first + i * spb + j
    return pl.pallas_call(
        functools.partial(_combine_body, n_steps=n_steps),
        grid=(b, spb),
        in_specs=[pl.BlockSpec((grp * CTAB_BLOCK,), lambda i, j: (step(i, j),),
                               memory_space=pltpu.SMEM),
                  pl.BlockSpec((grp * CTAB_BLOCK,),
                               lambda i, j: (jnp.minimum(step(i, j) + 1, first + n_steps - 1),),
                               memory_space=pltpu.SMEM),
                  pl.BlockSpec(memory_space=pl.ANY),
                  tok(LANES), tok(d),
                  pl.BlockSpec((1, 6, d), lambda i, j: (i, 0, 0)),
                  pl.BlockSpec((1, d), lambda i, j: (0, 0))],
        out_specs=tok(d),
        out_shape=jax.ShapeDtypeStruct((b, s, d), F32),
        scratch_shapes=[pltpu.VMEM((2 * grp, LOCAL_ROWS, d), F32), pltpu.SemaphoreType.DMA((2,))],
        compiler_params=_cparams("arbitrary", "arbitrary"),
        name="moe_combine",
    )(ctab, ctab, ys, route, x1, mod, g_post.reshape(1, d))


def _rope_tables(s_max):
    half = ROT_DIM // 2
    inv = jnp.power(jnp.float32(ROPE_THETA), -jnp.arange(0, ROT_DIM, 2, dtype=F32) / ROT_DIM)
    ang = jnp.arange(s_max, dtype=F32)[:, None] * inv[None, :]
    cos, sin = jnp.cos(ang), jnp.sin(ang)
    d = np.arange(LANES) % HEAD_DIM
    lo = jnp.asarray(d < half)
    hi = jnp.asarray((d >= half) & (d < ROT_DIM))
    cos_l = cos[:, d % half]
    sin_l = sin[:, d % half]
    c = jnp.where(lo | hi, cos_l, 1.0)
    s1 = jnp.where(lo, -sin_l, 0.0)
    s2 = jnp.where(hi, sin_l, 0.0)
    return jnp.stack([c, s1, s2]).astype(F32)


def _route_plan(blk_p, blk_s, counts, n_assign):
    blk = jnp.concatenate([blk_p.reshape(-1, 8, LANES), blk_s.reshape(-1, 8, LANES)], axis=0)
    n_blk = blk[:, 0, :N_EXPERTS].astype(jnp.int32)
    before = blk[:, 1, :N_EXPERTS].astype(jnp.int32)
    cnt = counts[0, :N_EXPERTS].astype(jnp.int32)
    tiles = (cnt + (MOE_TILE - 1)) // MOE_TILE
    tile_end = jnp.cumsum(tiles)
    start = (tile_end - tiles) * MOE_TILE
    n_pad = blk.shape[0] * N_EXPERTS * (ROW_CHUNK - 1)
    n_bound = (n_assign + n_pad) // MOE_TILE + N_EXPERTS
    n_tiles = n_bound + (IN_FLIGHT_BLOCKS * LOCAL_ROWS + MOE_TILE - 1) // MOE_TILE
    experts = jnp.arange(N_EXPERTS, dtype=jnp.int32)
    chunks = (n_blk + (ROW_CHUNK - 1)) // ROW_CHUNK
    chunk_end = jnp.cumsum(chunks, axis=1)
    j = jnp.arange(LOCAL_CHUNKS, dtype=jnp.int32)
    e = jnp.sum(chunk_end[:, None, :] <= j[None, :, None], axis=-1)
    pick = lambda table: jnp.sum(jnp.where(e[..., None] == experts, table[:, None, :], 0), axis=-1)
    in_segment = pick(start[None, :] + before) + ROW_CHUNK * (j[None, :] - pick(chunk_end - chunks))
    parity = (jnp.arange(blk.shape[0], dtype=jnp.int32) % IN_FLIGHT_BLOCKS)[:, None]
    scratch_row = n_bound * MOE_TILE + parity * LOCAL_ROWS + ROW_CHUNK * j[None, :]
    ctab = jnp.where(e < N_EXPERTS, in_segment, scratch_row)
    ctab = jnp.pad(ctab, ((0, 0), (0, CTAB_BLOCK - LOCAL_CHUNKS))).reshape(-1).astype(jnp.int32)
    tile_id = jnp.arange(n_tiles, dtype=jnp.int32)
    tile_expert = jnp.minimum(jnp.sum(tile_end[None, :] <= tile_id[:, None], axis=-1),
                              N_EXPERTS - 1).astype(jnp.int32)
    n_used = tile_end[-1:].astype(jnp.int32)
    zero0 = start + cnt
    zero_chunks = (tile_end * MOE_TILE - zero0) // ROW_CHUNK
    tail = jnp.concatenate([zero0, zero_chunks, n_used,
                            jnp.zeros((LANES - 2 * N_EXPERTS - 1,), jnp.int32)])
    return ctab, tail.astype(jnp.int32), tile_expert, n_used, n_tiles * MOE_TILE


def kernel(x_prompt, x_sample, c_prompt, c_sample, ada_w, ada_b, g_pre_mix, g_post_mix, g_pre_ffn,
           g_post_ffn, w_in, sink_a, rpb_b, w_branch_a, w_branch_b, w_out, w_router, b_router,
           w_gate, b_gate, w_up, b_up, w_down, b_down):
    d = D_MODEL
    scale = HEAD_DIM ** -0.5
    g = A_HEADS // A_KV_HEADS
    w = w_in[0]
    col_scale = np.ones((w.shape[1],), np.float32)
    col_scale[:A_Q] = scale
    qb0 = A_Q + 2 * A_KV
    col_scale[qb0:qb0 + B_W] = scale
    wq = w[:, :A_Q].reshape(d, A_KV_HEADS, g, HEAD_DIM).transpose(0, 2, 1, 3).reshape(d, A_Q)
    w_in_bf = (jnp.concatenate([wq, w[:, A_Q:]], axis=1) * col_scale).astype(BF16)
    wa_bf = w_branch_a[0].reshape(A_KV_HEADS, g, HEAD_DIM, d).transpose(1, 0, 2, 3).reshape(
        A_Q, d).astype(BF16)
    wb_bf = w_branch_b[0].astype(BF16)
    wo_bf = w_out[0].astype(BF16)
    wr_bf = jnp.zeros((d, LANES), F32).at[:, :N_EXPERTS].set(w_router[0]).astype(BF16)
    br = jnp.full((1, LANES), NEG, F32).at[0, :N_EXPERTS].set(b_router[0])
    wg_bf, wu_bf, wd_bf = w_gate[0], w_up[0], w_down[0]
    bg, bu, bd = (t[0].reshape(N_EXPERTS, 1, -1) for t in (b_gate, b_up, b_down))
    sink_rows = jnp.repeat(sink_a[0].astype(F32).reshape(A_KV_HEADS, g).T, A_BLOCK,
                           axis=1).reshape(g, 1, A_KV_HEADS * A_BLOCK)

    nb_p = x_prompt.shape[0]
    mod = _modulation(jnp.concatenate([c_prompt, c_sample], axis=0), ada_w[0], ada_b[0])
    mod = mod.reshape(-1, 6, d)
    s_max = max(x_prompt.shape[1], x_sample.shape[1])
    rope = _rope_tables(s_max)
    bias = _nbr_bias(rpb_b[0])

    def front(x, m, cnt_in):
        qa, ka, va, qb, kb, vb, ga, gb = _in_projection(x, m, g_pre_mix[0], w_in_bf, rope, 512)
        aa = _window_attention(qa, ka, va, sink_rows, 1024)
        ab = _nbr_attention(qb, kb, vb, bias, 1024)
        return _mix_and_route(x, aa, ab, ga, gb, m, wa_bf, wb_bf, wo_bf,
                              g_post_mix[0].reshape(1, d), g_pre_ffn[0].reshape(1, d),
                              wr_bf, br, cnt_in, 2 * TOKEN_BLOCK)

    mod_p, mod_s = mod[:nb_p], mod[nb_p:]
    x1p, h2p, route_p, blk_p, cnt_p = front(x_prompt, mod_p, jnp.zeros((1, LANES), F32))
    x1s, h2s, route_s, blk_s, cnt = front(x_sample, mod_s, cnt_p)

    h2p, h2s = h2p.reshape(-1, d), h2s.reshape(-1, d)
    n_assign = (h2p.shape[0] + h2s.shape[0]) * TOP_K
    ctab, tail, tile_expert, n_used, n_sorted = _route_plan(blk_p, blk_s, cnt, n_assign)

    def by_block(route):
        return route[..., :2 * TOP_K].reshape(-1, TOKEN_BLOCK, 2 * TOP_K).transpose(0, 2, 1)

    rt = jnp.concatenate([by_block(route_p), by_block(route_s)], axis=0)
    xs = _dispatch(ctab, tail, rt, h2p, h2s, n_sorted)
    ys = _grouped_ffn(tile_expert, n_used, xs, wg_bf, bg, wu_bf, bu, wd_bf, bd)
    y_p = _combine(ctab, 0, ys, route_p, x1p, mod_p, g_post_ffn[0])
    y_s = _combine(ctab, h2p.shape[0] // TOKEN_BLOCK, ys, route_s, x1s, mod_s, g_post_ffn[0])
    return (y_p, y_s)
```

```python
import functools

import jax
import jax.numpy as jnp
import numpy as np
from jax import lax
from jax.experimental import pallas as pl
from jax.experimental.pallas import tpu as pltpu

D_MODEL = 1024
HEAD_DIM = 64
ROT_DIM = HEAD_DIM // 4
ROPE_THETA = 500000.0
A_HEADS = 8
A_KV_HEADS = 2
WINDOW = 128
A_BLOCK = 128
B_HEADS = 8
GRID_W = 64
NA_ROWS = 8
NA_COLS = 16
N_EXPERTS = 32
TOP_K = 4
SWIGLU_LIMIT = 7.0
SWIGLU_ALPHA = 1.702
RMS_EPS = 1e-6
NEG = -1e30
A_Q = A_HEADS * HEAD_DIM
A_KV = A_KV_HEADS * HEAD_DIM
B_W = B_HEADS * HEAD_DIM

LANES = 128
PAIR = 2 * HEAD_DIM
N_PAIRS = B_W // PAIR
MOE_TILE = 512
TOKEN_BLOCK = 256
ROW_CHUNK_LOG2 = 3
ROW_CHUNK = 1 << ROW_CHUNK_LOG2
LOCAL_ROWS = TOKEN_BLOCK * TOP_K + N_EXPERTS * ROW_CHUNK
LOCAL_CHUNKS = LOCAL_ROWS // ROW_CHUNK
PIECE_CHUNKS = 4
MAX_BIG = LOCAL_CHUNKS // PIECE_CHUNKS
PT_BIG = 8
PT_SMALL = 96
CTAB_BLOCK = 4 * LANES
DISPATCH_GROUP = 2
IN_FLIGHT_BLOCKS = 2 * DISPATCH_GROUP
VMEM_LIMIT = 56 * 1024 * 1024

BF16 = jnp.bfloat16
F32 = jnp.float32


def _cparams(*sem):
    return pltpu.CompilerParams(dimension_semantics=sem, vmem_limit_bytes=VMEM_LIMIT)


def _mod_body(c_ref, w_ref, b_ref, o_ref):
    c = c_ref[...]
    s = c * (1.0 / (1.0 + jnp.exp(-c)))
    o_ref[...] = jnp.dot(s, w_ref[...], preferred_element_type=F32,
                         precision=lax.Precision.HIGHEST) + b_ref[...]


def _modulation(c, ada_w, ada_b):
    n, d = c.shape
    blk = 1024
    return pl.pallas_call(
        _mod_body,
        grid=(ada_w.shape[1] // blk,),
        in_specs=[pl.BlockSpec((n, d), lambda j: (0, 0)),
                  pl.BlockSpec((d, blk), lambda j: (0, j)),
                  pl.BlockSpec((1, blk), lambda j: (0, j))],
        out_specs=pl.BlockSpec((n, blk), lambda j: (0, j)),
        out_shape=jax.ShapeDtypeStruct((n, ada_w.shape[1]), F32),
        compiler_params=_cparams("arbitrary"),
        name="adaln_mod",
    )(c, ada_w, ada_b.reshape(1, -1))


def _rms(x, g):
    return x * lax.rsqrt(jnp.mean(x * x, axis=-1, keepdims=True) + RMS_EPS) * g


def _rope_tile(seg, rope_ref):
    return (seg * rope_ref[0] + pltpu.roll(seg, LANES - ROT_DIM // 2, 1) * rope_ref[1]
            + pltpu.roll(seg, ROT_DIM // 2, 1) * rope_ref[2])


def _inproj_body(x_ref, mod_ref, g_ref, w_ref, rope_ref,
                 qa_ref, ka_ref, va_ref, qb_ref, kb_ref, vb_ref, ga_ref, gb_ref):
    sh1 = mod_ref[0, 0:1, :]
    sc1 = mod_ref[0, 1:2, :]
    h = (_rms(x_ref[0], g_ref[...]) * (1.0 + sc1) + sh1).astype(BF16)

    def proj(lo, hi):
        return jnp.dot(h, w_ref[:, lo:hi], preferred_element_type=F32)

    qa = proj(0, A_Q)
    for t in range(A_Q // LANES):
        qa_ref[0, :, t * LANES:(t + 1) * LANES] = _rope_tile(
            qa[:, t * LANES:(t + 1) * LANES], rope_ref).astype(BF16)
    o = A_Q
    kva = proj(o, o + 2 * A_KV)
    ka_ref[0] = _rope_tile(kva[:, :A_KV], rope_ref).astype(BF16)
    va_ref[0] = kva[:, A_KV:].astype(BF16)
    o += 2 * A_KV
    for ref in (qb_ref, kb_ref, vb_ref):
        ref[0] = proj(o, o + B_W).astype(BF16)
        o += B_W
    for ref in (ga_ref, gb_ref):
        ref[0] = proj(o, o + D_MODEL).astype(BF16)
        o += D_MODEL


def _in_projection(x, mod, g_pre, w_in_bf, rope, tm):
    b, s, d = x.shape
    widths = (A_Q, A_KV, A_KV, B_W, B_W, B_W, D_MODEL, D_MODEL)
    tok = lambda w: pl.BlockSpec((1, tm, w), lambda i, j: (i, j, 0))
    return pl.pallas_call(
        _inproj_body,
        grid=(b, s // tm),
        in_specs=[tok(d),
                  pl.BlockSpec((1, 6, d), lambda i, j: (i, 0, 0)),
                  pl.BlockSpec((1, d), lambda i, j: (0, 0)),
                  pl.BlockSpec(w_in_bf.shape, lambda i, j: (0, 0)),
                  pl.BlockSpec((3, tm, LANES), lambda i, j: (0, j, 0))],
        out_specs=[tok(w) for w in widths],
        out_shape=[jax.ShapeDtypeStruct((b, s, w), BF16) for w in widths],
        compiler_params=_cparams("arbitrary", "arbitrary"),
        name="in_projection",
    )(x, mod, g_pre.reshape(1, d), w_in_bf, rope)


def _stack_heads(t):
    lane = lax.broadcasted_iota(jnp.int32, t.shape, 1)
    zero = jnp.zeros_like(t)
    return jnp.concatenate([jnp.where(lane < HEAD_DIM, t, zero),
                            jnp.where(lane >= HEAD_DIM, t, zero)], axis=0)


def _window_body(q_ref, k_ref, v_ref, sink_ref, band_ref, o_ref, *, seq, tq):
    span = A_BLOCK + 2 * WINDOW
    lane_o = lax.broadcasted_iota(jnp.int32, (A_BLOCK, LANES), 1)
    for sub in range(tq // A_BLOCK):
        q0 = pl.program_id(1) * tq + sub * A_BLOCK
        k0 = pl.multiple_of(jnp.clip(q0 - WINDOW, 0, seq - span), A_BLOCK)
        kw = k_ref[0, pl.ds(k0, span), :]
        vw = v_ref[0, pl.ds(k0, span), :]
        band = band_ref[lax.div(q0 - k0, A_BLOCK)]
        for pr in range(N_PAIRS):
            qs = _stack_heads(
                q_ref[0, sub * A_BLOCK:(sub + 1) * A_BLOCK, pr * LANES:(pr + 1) * LANES])
            st = lax.dot_general(kw, qs, (((1,), (1,)), ((), ())), preferred_element_type=F32) + band
            snk = sink_ref[pr]
            m = jnp.maximum(jnp.max(st, axis=0, keepdims=True), snk)
            e = jnp.exp(st - m)
            inv = 1.0 / (jnp.sum(e, axis=0, keepdims=True) + jnp.exp(snk - m))
            o = lax.dot_general((e * inv).astype(BF16), vw, (((0,), (0,)), ((), ())),
                                preferred_element_type=F32)
            o_ref[0, sub * A_BLOCK:(sub + 1) * A_BLOCK, pr * LANES:(pr + 1) * LANES] = jnp.where(
                lane_o < HEAD_DIM, o[:A_BLOCK], o[A_BLOCK:]).astype(BF16)


def _window_attention(q, k, v, sink_rows, tq):
    b, s, _ = q.shape
    span = A_BLOCK + 2 * WINDOW
    off = np.arange(3)[:, None, None] * A_BLOCK
    kj = np.arange(span)[None, :, None]
    qi = (np.arange(A_KV_HEADS * A_BLOCK) % A_BLOCK)[None, None, :]
    band = jnp.asarray(np.where(np.abs(qi + off - kj) <= WINDOW, 0.0, NEG).astype(np.float32))
    return pl.pallas_call(
        functools.partial(_window_body, seq=s, tq=tq),
        grid=(b, s // tq),
        in_specs=[pl.BlockSpec((1, tq, A_Q), lambda i, j: (i, j, 0)),
                  pl.BlockSpec((1, s, A_KV), lambda i, j: (i, 0, 0)),
                  pl.BlockSpec((1, s, A_KV), lambda i, j: (i, 0, 0)),
                  pl.BlockSpec(sink_rows.shape, lambda i, j: (0, 0, 0)),
                  pl.BlockSpec(band.shape, lambda i, j: (0, 0, 0))],
        out_specs=pl.BlockSpec((1, tq, A_Q), lambda i, j: (i, j, 0)),
        out_shape=jax.ShapeDtypeStruct((b, s, A_Q), BF16),
        compiler_params=_cparams("arbitrary", "arbitrary"),
        name="window_attention",
    )(q, k, v, sink_rows, band)


def _nbr_bias_body(rpb_ref, o_ref):
    p = pl.program_id(0)
    d = pl.program_id(1)
    nkeys = NA_ROWS * GRID_W
    row = lax.broadcasted_iota(jnp.int32, (2 * GRID_W, nkeys), 0)
    col = lax.broadcasted_iota(jnp.int32, (2 * GRID_W, nkeys), 1)
    qc = row % GRID_W
    kc = col % GRID_W
    qstart = jnp.clip(qc - NA_COLS // 2, 0, GRID_W - NA_COLS)
    valid = (kc >= qstart) & (kc < qstart + NA_COLS)
    cidx = kc - qc + NA_COLS - 1
    key_row = lax.broadcasted_iota(jnp.int32, (1, nkeys), 1) // GRID_W
    acc = jnp.full((2 * GRID_W, nkeys), NEG, F32)
    for j in range(2 * NA_COLS - 1):
        rvs = []
        for hh in range(2):
            rv = jnp.zeros((1, nkeys), F32)
            for aa in range(NA_ROWS):
                rv = jnp.where(key_row == aa, rpb_ref[2 * p + hh, aa - d + NA_ROWS - 1, j], rv)
            rvs.append(rv)
        acc = jnp.where(valid & (cidx == j), jnp.where(row < GRID_W, rvs[0], rvs[1]), acc)
    o_ref[0, 0] = acc.T


def _nbr_bias(rpb):
    nkeys = NA_ROWS * GRID_W
    return pl.pallas_call(
        _nbr_bias_body,
        grid=(N_PAIRS, NA_ROWS),
        in_specs=[pl.BlockSpec(memory_space=pltpu.SMEM)],
        out_specs=pl.BlockSpec((1, 1, nkeys, 2 * GRID_W), lambda p, d: (p, d, 0, 0)),
        out_shape=jax.ShapeDtypeStruct((N_PAIRS, NA_ROWS, nkeys, 2 * GRID_W), F32),
        compiler_params=_cparams("arbitrary", "arbitrary"),
        name="nbr_bias",
    )(rpb)


def _nbr_body(q_ref, k_ref, v_ref, bias_ref, o_ref, *, seq, tq):
    n_rows = seq // GRID_W
    nkeys = NA_ROWS * GRID_W
    lane_o = lax.broadcasted_iota(jnp.int32, (GRID_W, LANES), 1)

    for rl in range(tq // GRID_W):
        r = pl.program_id(1) * (tq // GRID_W) + rl
        r0 = jnp.clip(r - NA_ROWS // 2, 0, n_rows - NA_ROWS)
        d = r - r0
        k0 = pl.multiple_of(r0 * GRID_W, GRID_W)
        q0 = rl * GRID_W
        for pr in range(N_PAIRS):
            lanes = slice(pr * LANES, (pr + 1) * LANES)
            kw = k_ref[0, pl.ds(k0, nkeys), lanes]
            vw = v_ref[0, pl.ds(k0, nkeys), lanes]
            qs = _stack_heads(q_ref[0, pl.ds(q0, GRID_W), lanes])
            st = lax.dot_general(kw, qs, (((1,), (1,)), ((), ())), preferred_element_type=F32)
            st = st + bias_ref[pr, d]
            m = jnp.max(st, axis=0, keepdims=True)
            e = jnp.exp(st - m)
            inv = 1.0 / jnp.sum(e, axis=0, keepdims=True)
            o = lax.dot_general((e * inv).astype(BF16), vw, (((0,), (0,)), ((), ())),
                                preferred_element_type=F32)
            o_ref[0, pl.ds(q0, GRID_W), lanes] = jnp.where(
                lane_o < HEAD_DIM, o[:GRID_W], o[GRID_W:]).astype(BF16)


def _nbr_attention(q, k, v, bias, tq):
    b, s, _ = q.shape
    return pl.pallas_call(
        functools.partial(_nbr_body, seq=s, tq=tq),
        grid=(b, s // tq),
        in_specs=[pl.BlockSpec((1, tq, B_W), lambda i, j: (i, j, 0)),
                  pl.BlockSpec((1, s, B_W), lambda i, j: (i, 0, 0)),
                  pl.BlockSpec((1, s, B_W), lambda i, j: (i, 0, 0)),
                  pl.BlockSpec(bias.shape, lambda i, j: (0, 0, 0, 0))],
        out_specs=pl.BlockSpec((1, tq, B_W), lambda i, j: (i, j, 0)),
        out_shape=jax.ShapeDtypeStruct((b, s, B_W), BF16),
        compiler_params=_cparams("arbitrary", "arbitrary"),
        name="nbr_attention",
    )(q, k, v, bias)


def _sigmoid(z):
    return 1.0 / (1.0 + jnp.exp(-z))


def _mix_body(x_ref, aa_ref, ab_ref, ga_ref, gb_ref, mod_ref, wa_ref, wb_ref, wo_ref,
              gpost_ref, gpre_ref, wr_ref, br_ref, cnt_in_ref,
              x1_ref, h2_ref, route_ref, blk_ref, cnt_ref, run_ref, *, tm):
    first = (pl.program_id(0) == 0) & (pl.program_id(1) == 0)

    @pl.when(first)
    def _():
        run_ref[...] = cnt_in_ref[...]

    ya = jnp.dot(aa_ref[0], wa_ref[...], preferred_element_type=F32)
    yb = jnp.dot(ab_ref[0], wb_ref[...], preferred_element_type=F32)
    merged = _sigmoid(ga_ref[0].astype(F32)) * ya + _sigmoid(gb_ref[0].astype(F32)) * yb
    z = jnp.dot(merged.astype(BF16), wo_ref[...], preferred_element_type=F32)
    gt1 = mod_ref[0, 2:3, :]
    sh2 = mod_ref[0, 3:4, :]
    sc2 = mod_ref[0, 4:5, :]
    x1 = x_ref[0] + gt1 * _rms(z, gpost_ref[...])
    x1_ref[0] = x1
    h2 = _rms(x1, gpre_ref[...]) * (1.0 + sc2) + sh2
    h2_bf = h2.astype(BF16)
    h2_ref[0] = h2_bf

    logits = jnp.dot(h2_bf, wr_ref[...], preferred_element_type=F32) + br_ref[...]
    lane = lax.broadcasted_iota(jnp.int32, (tm, LANES), 1).astype(F32)
    work = logits
    sel = jnp.zeros((tm, LANES), F32)
    vals, idxs = [], []
    for _ in range(TOP_K):
        m = jnp.max(work, axis=-1, keepdims=True)
        idx = jnp.min(jnp.where(work == m, lane, float(LANES)), axis=-1, keepdims=True)
        hit = lane == idx
        vals.append(m)
        idxs.append(idx)
        work = jnp.where(hit, -jnp.inf, work)
        sel = sel + hit.astype(F32)
    es = [jnp.exp(v - vals[0]) for v in vals]
    den = es[0] + es[1] + es[2] + es[3]

    tb = TOKEN_BLOCK
    ri = lax.broadcasted_iota(jnp.int32, (tb, tb), 0)
    ci = lax.broadcasted_iota(jnp.int32, (tb, tb), 1)
    tri = (ci < ri).astype(BF16)
    li = lax.broadcasted_iota(jnp.int32, (LANES, LANES), 0)
    lj = lax.broadcasted_iota(jnp.int32, (LANES, LANES), 1)
    upper = (li < lj).astype(BF16)
    row = lax.broadcasted_iota(jnp.int32, (8, LANES), 0)
    lane_b = lax.broadcasted_iota(jnp.int32, (tb, LANES), 1)
    wts = jnp.zeros((tm, LANES), F32)
    for k in range(TOP_K):
        wts = jnp.where(lane == k, es[k] / den, wts)
    hits = [jnp.where(lane == idxs[k], 1.0, 0.0) for k in range(TOP_K)]
    for sb in range(tm // tb):
        rs = slice(sb * tb, (sb + 1) * tb)
        sel_b = sel[rs]
        earlier = jnp.dot(tri, sel_b.astype(BF16), preferred_element_type=F32)
        n_blk = jnp.sum(sel_b, axis=0, keepdims=True)
        chunks = jnp.floor((n_blk + (ROW_CHUNK - 1.0)) * (1.0 / ROW_CHUNK))
        seg_start = ROW_CHUNK * jnp.dot(jnp.broadcast_to(chunks, (8, LANES)).astype(BF16), upper,
                                        preferred_element_type=F32)[0:1, :]
        slot = earlier + seg_start
        out = wts[rs]
        for k in range(TOP_K):
            local = jnp.sum(hits[k][rs] * slot, axis=-1, keepdims=True)
            out = jnp.where(lane_b == TOP_K + k, local, out)
        route_ref[0, rs, :] = out
        blk_ref[0, sb * 8:(sb + 1) * 8, :] = jnp.where(
            row == 0, n_blk, jnp.where(row == 1, run_ref[...], jnp.where(row == 2, seg_start, 0.0)))
        run_ref[...] = run_ref[...] + ROW_CHUNK * chunks
    cnt_ref[...] = run_ref[...]


def _mix_and_route(x, aa, ab, ga, gb, mod, wa, wb, wo, gpost, gpre, wr, br, cnt_in, tm):
    b, s, d = x.shape
    tok = lambda w: pl.BlockSpec((1, tm, w), lambda i, j: (i, j, 0))
    full = lambda a: pl.BlockSpec(a.shape, lambda i, j: (0,) * a.ndim)
    return pl.pallas_call(
        functools.partial(_mix_body, tm=tm),
        grid=(b, s // tm),
        in_specs=[tok(d), tok(A_Q), tok(B_W), tok(d), tok(d),
                  pl.BlockSpec((1, 6, d), lambda i, j: (i, 0, 0)),
                  full(wa), full(wb), full(wo), full(gpost), full(gpre), full(wr), full(br),
                  full(cnt_in)],
        out_specs=[tok(d), tok(d), tok(LANES),
                   pl.BlockSpec((1, (tm // TOKEN_BLOCK) * 8, LANES), lambda i, j: (i, j, 0)),
                   pl.BlockSpec((1, LANES), lambda i, j: (0, 0))],
        out_shape=[jax.ShapeDtypeStruct((b, s, d), F32), jax.ShapeDtypeStruct((b, s, d), BF16),
                   jax.ShapeDtypeStruct((b, s, LANES), F32),
                   jax.ShapeDtypeStruct((b, (s // TOKEN_BLOCK) * 8, LANES), F32),
                   jax.ShapeDtypeStruct((1, LANES), F32)],
        scratch_shapes=[pltpu.VMEM((1, LANES), F32)],
        compiler_params=_cparams("arbitrary", "arbitrary"),
        name="mix_and_route",
    )(x, aa, ab, ga, gb, mod, wa, wb, wo, gpost, gpre, wr, br, cnt_in)


def _piece_loop(tab_ref, base, fn):
    def big(k, carry):
        fn(pl.multiple_of(tab_ref[base + PT_BIG + k] * ROW_CHUNK, ROW_CHUNK),
           pl.multiple_of(tab_ref[base + PT_BIG + MAX_BIG + k], ROW_CHUNK), PIECE_CHUNKS * ROW_CHUNK)
        return carry

    lax.fori_loop(0, tab_ref[base], big, 0)

    def small(k, carry):
        fn(pl.multiple_of(tab_ref[base + PT_SMALL + k] * ROW_CHUNK, ROW_CHUNK),
           pl.multiple_of(tab_ref[base + PT_SMALL + LOCAL_CHUNKS + k], ROW_CHUNK), ROW_CHUNK)
        return carry

    lax.fori_loop(0, tab_ref[base + 1], small, 0)


def _dispatch_body(ctab_ref, tail_ref, rt_ref, hp_ref, hs_ref, xs_ref, loc_ref, zero_ref, sem,
                   *, np_steps, n_steps):
    i = pl.program_id(0)
    slot = i % 2
    d = D_MODEL
    grp = DISPATCH_GROUP

    def full_wait(s):
        for b in range(grp):
            pltpu.make_async_copy(loc_ref.at[s * grp + b], xs_ref.at[pl.ds(0, LOCAL_ROWS)],
                                  sem.at[s]).wait()

    @pl.when(i == 0)
    def _():
        zero_ref[...] = jnp.zeros_like(zero_ref)

        def chunk_copy(s):
            return pltpu.make_async_copy(zero_ref.at[pl.ds(0, ROW_CHUNK)],
                                         xs_ref.at[pl.ds(s, ROW_CHUNK)], sem.at[0])

        def tails(fn):
            def per_expert(e, carry):
                def chunk(c, carry2):
                    fn(chunk_copy(pl.multiple_of(tail_ref[e] + c * ROW_CHUNK, ROW_CHUNK)))
                    return carry2

                lax.fori_loop(0, tail_ref[N_EXPERTS + e], chunk, 0)
                return carry

            lax.fori_loop(0, N_EXPERTS, per_expert, 0)

        tails(lambda cp: cp.start())
        tails(lambda cp: cp.wait())

        def unused(fn):
            def tile(t, carry):
                r = pl.multiple_of(t * MOE_TILE, MOE_TILE)
                fn(pltpu.make_async_copy(zero_ref, xs_ref.at[pl.ds(r, MOE_TILE)], sem.at[0]))
                return carry

            lax.fori_loop(tail_ref[2 * N_EXPERTS], xs_ref.shape[0] // MOE_TILE, tile, 0)

        unused(lambda cp: cp.start())
        unused(lambda cp: cp.wait())

    @pl.when(i >= 2)
    def _():
        full_wait(slot)

    srow = lax.broadcasted_iota(jnp.int32, (LOCAL_ROWS, TOKEN_BLOCK), 0).astype(F32)
    for b in range(grp):
        buf = slot * grp + b
        rows = slice(b * TOKEN_BLOCK, (b + 1) * TOKEN_BLOCK)
        rt = rt_ref[b]
        hit_any = None
        wsel = jnp.zeros((LOCAL_ROWS, TOKEN_BLOCK), F32)
        for k in range(TOP_K):
            hit = srow == rt[TOP_K + k:TOP_K + k + 1, :]
            wsel = jnp.where(hit, rt[k:k + 1, :], wsel)
            hit_any = hit if hit_any is None else (hit_any | hit)
        onehot = jnp.where(hit_any, 1.0, 0.0).astype(BF16)
        h = jnp.where(i < np_steps, hp_ref[rows, :], hs_ref[rows, :])
        loc_ref[buf, :, 0:d] = jnp.dot(onehot, h, preferred_element_type=F32)
        loc_ref[buf, :, d:d + LANES] = jnp.broadcast_to(jnp.sum(wsel, axis=1, keepdims=True),
                                                        (LOCAL_ROWS, LANES))

    for b in range(grp):
        def start(l, s, n, b=b):
            pltpu.make_async_copy(loc_ref.at[slot * grp + b, pl.ds(l, n)], xs_ref.at[pl.ds(s, n)],
                                  sem.at[slot]).start()

        _piece_loop(ctab_ref, b * CTAB_BLOCK, start)

    @pl.when(i == n_steps - 1)
    def _():
        full_wait(slot)
        if n_steps > 1:
            full_wait(1 - slot)


def _dispatch(ctab, tail, rt, h2p, h2s, n_sorted):
    d = h2p.shape[1]
    grp = DISPATCH_GROUP
    rows = grp * TOKEN_BLOCK
    np_steps = h2p.shape[0] // rows
    ns_steps = h2s.shape[0] // rows
    return pl.pallas_call(
        functools.partial(_dispatch_body, np_steps=np_steps, n_steps=np_steps + ns_steps),
        grid=(np_steps + ns_steps,),
        in_specs=[pl.BlockSpec((grp * CTAB_BLOCK,), lambda i: (i,), memory_space=pltpu.SMEM),
                  pl.BlockSpec((LANES,), lambda i: (0,), memory_space=pltpu.SMEM),
                  pl.BlockSpec((grp, 8, TOKEN_BLOCK), lambda i: (i, 0, 0)),
                  pl.BlockSpec((rows, d), lambda i: (jnp.minimum(i, np_steps - 1), 0)),
                  pl.BlockSpec((rows, d), lambda i: (jnp.clip(i - np_steps, 0, ns_steps - 1), 0))],
        out_specs=pl.BlockSpec(memory_space=pl.ANY),
        out_shape=jax.ShapeDtypeStruct((n_sorted, d + LANES), F32),
        scratch_shapes=[pltpu.VMEM((2 * grp, LOCAL_ROWS, d + LANES), F32),
                        pltpu.VMEM((MOE_TILE, d + LANES), F32),
                        pltpu.SemaphoreType.DMA((2,))],
        compiler_params=_cparams("arbitrary"),
        name="moe_dispatch",
    )(ctab, tail, rt, h2p, h2s)


def _gmm_body(te_ref, nu_ref, x_ref, wg_ref, bg_ref, wu_ref, bu_ref, wd_ref, bd_ref, y_ref,
              wg_bf, wu_bf, wd_bf):
    i = pl.program_id(0)
    d = D_MODEL
    used = i < nu_ref[0]

    @pl.when(used & ((i == 0) | (te_ref[i] != te_ref[jnp.maximum(i - 1, 0)])))
    def _():
        wg_bf[...] = wg_ref[0].astype(BF16)
        wu_bf[...] = wu_ref[0].astype(BF16)
        wd_bf[...] = wd_ref[0].astype(BF16)

    @pl.when(used)
    def _():
        x = x_ref[:, 0:d].astype(BF16)
        g = jnp.minimum(jnp.dot(x, wg_bf[...], preferred_element_type=F32) + bg_ref[0], SWIGLU_LIMIT)
        u = jnp.clip(jnp.dot(x, wu_bf[...], preferred_element_type=F32) + bu_ref[0],
                     -SWIGLU_LIMIT, SWIGLU_LIMIT)
        act = (u + 1.0) * (g * _sigmoid(SWIGLU_ALPHA * g))
        y = jnp.dot(act.astype(BF16), wd_bf[...], preferred_element_type=F32) + bd_ref[0]
        y_ref[...] = x_ref[:, d:d + 1] * y

    @pl.when(i >= nu_ref[0])
    def _():
        y_ref[...] = jnp.zeros_like(y_ref)


def _grouped_ffn(tile_expert, n_used, xs, wg, bg, wu, bu, wd, bd):
    n = xs.shape[0]
    d, f = wg.shape[1], wg.shape[2]
    wspec = lambda a, b_: pl.BlockSpec((1, a, b_), lambda i, te, nu: (te[i], 0, 0))
    return pl.pallas_call(
        _gmm_body,
        grid_spec=pltpu.PrefetchScalarGridSpec(
            num_scalar_prefetch=2,
            grid=(n // MOE_TILE,),
            in_specs=[pl.BlockSpec((MOE_TILE, d + LANES),
                                   lambda i, te, nu: (jnp.minimum(i, nu[0] - 1), 0)),
                      wspec(d, f), wspec(1, f), wspec(d, f), wspec(1, f), wspec(f, d), wspec(1, d)],
            out_specs=pl.BlockSpec((MOE_TILE, d), lambda i, te, nu: (i, 0)),
            scratch_shapes=[pltpu.VMEM((d, f), BF16), pltpu.VMEM((d, f), BF16),
                            pltpu.VMEM((f, d), BF16)],
        ),
        out_shape=jax.ShapeDtypeStruct((n, d), F32),
        compiler_params=_cparams("arbitrary"),
        name="moe_grouped_ffn",
    )(tile_expert, n_used, xs, wg, bg, wu, bu, wd, bd)


def _combine_body(ctab_ref, cnext_ref, ys_ref, route_ref, x1_ref, mod_ref, g_ref, o_ref, loc_ref, sem,
                  *, n_steps):
    t = pl.program_id(0) * pl.num_programs(1) + pl.program_id(1)
    slot = t % 2
    grp = DISPATCH_GROUP

    def fetch(tab_ref, s):
        for b in range(grp):
            def start(l, r, n, b=b):
                pltpu.make_async_copy(ys_ref.at[pl.ds(r, n)], loc_ref.at[s * grp + b, pl.ds(l, n)],
                                      sem.at[s]).start()

            _piece_loop(tab_ref, b * CTAB_BLOCK, start)

    @pl.when(t == 0)
    def _():
        fetch(ctab_ref, slot)

    @pl.when(t + 1 < n_steps)
    def _():
        fetch(cnext_ref, 1 - slot)

    for b in range(grp):
        pltpu.make_async_copy(ys_ref.at[pl.ds(0, LOCAL_ROWS)], loc_ref.at[slot * grp + b],
                              sem.at[slot]).wait()

    gt2 = mod_ref[0, 5:6, :]
    col = lax.broadcasted_iota(jnp.int32, (TOKEN_BLOCK, LOCAL_ROWS), 1).astype(F32)
    for b in range(grp):
        rows = slice(b * TOKEN_BLOCK, (b + 1) * TOKEN_BLOCK)
        route = route_ref[0, rows, :]
        hit_any = None
        for k in range(TOP_K):
            hit = col == route[:, TOP_K + k:TOP_K + k + 1]
            hit_any = hit if hit_any is None else (hit_any | hit)
        onehot = jnp.where(hit_any, 1.0, 0.0).astype(BF16)
        y2 = jnp.dot(onehot, loc_ref[slot * grp + b].astype(BF16), preferred_element_type=F32)
        o_ref[0, rows, :] = x1_ref[0, rows, :] + gt2 * _rms(y2, g_ref[...])


def _combine(ctab, first_block, ys, route, x1, mod, g_post):
    b, s, d = x1.shape
    grp = DISPATCH_GROUP
    rows = grp * TOKEN_BLOCK
    spb = s // rows
    n_steps = b * spb
    first = first_block // grp
    tok = lambda w: pl.BlockSpec((1, rows, w), lambda i, j: (i, j, 0))
    step = lambda i, j: first + i * spb + j
    return pl.pallas_call(
        functools.partial(_combine_body, n_steps=n_steps),
        grid=(b, spb),
        in_specs=[pl.BlockSpec((grp * CTAB_BLOCK,), lambda i, j: (step(i, j),),
                               memory_space=pltpu.SMEM),
                  pl.BlockSpec((grp * CTAB_BLOCK,),
                               lambda i, j: (jnp.minimum(step(i, j) + 1, first + n_steps - 1),),
                               memory_space=pltpu.SMEM),
                  pl.BlockSpec(memory_space=pl.ANY),
                  tok(LANES), tok(d),
                  pl.BlockSpec((1, 6, d), lambda i, j: (i, 0, 0)),
                  pl.BlockSpec((1, d), lambda i, j: (0, 0))],
        out_specs=tok(d),
        out_shape=jax.ShapeDtypeStruct((b, s, d), F32),
        scratch_shapes=[pltpu.VMEM((2 * grp, LOCAL_ROWS, d), F32), pltpu.SemaphoreType.DMA((2,))],
        compiler_params=_cparams("arbitrary", "arbitrary"),
        name="moe_combine",
    )(ctab, ctab, ys, route, x1, mod, g_post.reshape(1, d))


def _rope_tables(s_max):
    half = ROT_DIM // 2
    inv = jnp.power(jnp.float32(ROPE_THETA), -jnp.arange(0, ROT_DIM, 2, dtype=F32) / ROT_DIM)
    ang = jnp.arange(s_max, dtype=F32)[:, None] * inv[None, :]
    cos, sin = jnp.cos(ang), jnp.sin(ang)
    d = np.arange(LANES) % HEAD_DIM
    lo = jnp.asarray(d < half)
    hi = jnp.asarray((d >= half) & (d < ROT_DIM))
    cos_l = cos[:, d % half]
    sin_l = sin[:, d % half]
    c = jnp.where(lo | hi, cos_l, 1.0)
    s1 = jnp.where(lo, -sin_l, 0.0)
    s2 = jnp.where(hi, sin_l, 0.0)
    return jnp.stack([c, s1, s2]).astype(F32)


def _route_plan(blk_p, blk_s, counts, n_assign):
    blk = jnp.concatenate([blk_p.reshape(-1, 8, LANES), blk_s.reshape(-1, 8, LANES)], axis=0)
    n_blk = blk[:, 0, :N_EXPERTS].astype(jnp.int32)
    before = blk[:, 1, :N_EXPERTS].astype(jnp.int32)
    cnt = counts[0, :N_EXPERTS].astype(jnp.int32)
    tiles = (cnt + (MOE_TILE - 1)) // MOE_TILE
    tile_end = jnp.cumsum(tiles)
    start = (tile_end - tiles) * MOE_TILE
    n_pad = blk.shape[0] * N_EXPERTS * (ROW_CHUNK - 1)
    n_bound = (n_assign + n_pad) // MOE_TILE + N_EXPERTS
    n_tiles = n_bound + (IN_FLIGHT_BLOCKS * LOCAL_ROWS + MOE_TILE - 1) // MOE_TILE
    experts = jnp.arange(N_EXPERTS, dtype=jnp.int32)
    chunks = (n_blk + (ROW_CHUNK - 1)) // ROW_CHUNK
    chunk_end = jnp.cumsum(chunks, axis=1)
    j = jnp.arange(LOCAL_CHUNKS, dtype=jnp.int32)
    e = jnp.sum(chunk_end[:, None, :] <= j[None, :, None], axis=-1)
    pick = lambda table: jnp.sum(jnp.where(e[..., None] == experts, table[:, None, :], 0), axis=-1)
    in_segment = pick(start[None, :] + before) + ROW_CHUNK * (j[None, :] - pick(chunk_end - chunks))
    parity = (jnp.arange(blk.shape[0], dtype=jnp.int32) % IN_FLIGHT_BLOCKS)[:, None]
    scratch_row = n_bound * MOE_TILE + parity * LOCAL_ROWS + ROW_CHUNK * j[None, :]
    rows = jnp.where(e < N_EXPERTS, in_segment, scratch_row)
    used = chunk_end[:, -1:]
    seg_first = jnp.where(e < N_EXPERTS, pick(chunk_end - chunks), used)
    seg_len = jnp.where(e < N_EXPERTS, pick(chunks), LOCAL_CHUNKS - used)
    q = j[None, :] - seg_first
    in_big = q < (seg_len // PIECE_CHUNKS) * PIECE_CHUNKS
    big_start = in_big & (q % PIECE_CHUNKS == 0)
    first_of = lambda flag: jnp.sort(jnp.where(flag, j[None, :], LOCAL_CHUNKS + j[None, :]), axis=1)
    row_of = lambda idx: jnp.sum(jnp.where(idx[..., None] == j, rows[:, None, :], 0), axis=-1)
    big_j = first_of(big_start)[:, :MAX_BIG]
    small_j = first_of(~in_big)
    counts = jnp.stack([jnp.sum(big_start, axis=1), jnp.sum(~in_big, axis=1)], axis=1)
    pad = lambda a, w: jnp.pad(a, ((0, 0), (0, w - a.shape[1])))
    ctab = jnp.concatenate([pad(counts, PT_BIG), pad(big_j, MAX_BIG), pad(row_of(big_j), PT_SMALL - PT_BIG - MAX_BIG),
                            pad(small_j, LOCAL_CHUNKS), pad(row_of(small_j), CTAB_BLOCK - PT_SMALL - LOCAL_CHUNKS)],
                           axis=1).reshape(-1).astype(jnp.int32)
    tile_id = jnp.arange(n_tiles, dtype=jnp.int32)
    tile_expert = jnp.minimum(jnp.sum(tile_end[None, :] <= tile_id[:, None], axis=-1),
                              N_EXPERTS - 1).astype(jnp.int32)
    n_used = tile_end[-1:].astype(jnp.int32)
    zero0 = start + cnt
    zero_chunks = (tile_end * MOE_TILE - zero0) // ROW_CHUNK
    tail = jnp.concatenate([zero0, zero_chunks, n_used,
                            jnp.zeros((LANES - 2 * N_EXPERTS - 1,), jnp.int32)])
    return ctab, tail.astype(jnp.int32), tile_expert, n_used, n_tiles * MOE_TILE


def kernel(x_prompt, x_sample, c_prompt, c_sample, ada_w, ada_b, g_pre_mix, g_post_mix, g_pre_ffn,
           g_post_ffn, w_in, sink_a, rpb_b, w_branch_a, w_branch_b, w_out, w_router, b_router,
           w_gate, b_gate, w_up, b_up, w_down, b_down):
    d = D_MODEL
    scale = HEAD_DIM ** -0.5
    g = A_HEADS // A_KV_HEADS
    w = w_in[0]
    col_scale = np.ones((w.shape[1],), np.float32)
    col_scale[:A_Q] = scale
    qb0 = A_Q + 2 * A_KV
    col_scale[qb0:qb0 + B_W] = scale
    wq = w[:, :A_Q].reshape(d, A_KV_HEADS, g, HEAD_DIM).transpose(0, 2, 1, 3).reshape(d, A_Q)
    w_in_bf = (jnp.concatenate([wq, w[:, A_Q:]], axis=1) * col_scale).astype(BF16)
    wa_bf = w_branch_a[0].reshape(A_KV_HEADS, g, HEAD_DIM, d).transpose(1, 0, 2, 3).reshape(
        A_Q, d).astype(BF16)
    wb_bf = w_branch_b[0].astype(BF16)
    wo_bf = w_out[0].astype(BF16)
    wr_bf = jnp.zeros((d, LANES), F32).at[:, :N_EXPERTS].set(w_router[0]).astype(BF16)
    br = jnp.full((1, LANES), NEG, F32).at[0, :N_EXPERTS].set(b_router[0])
    wg_bf, wu_bf, wd_bf = w_gate[0], w_up[0], w_down[0]
    bg, bu, bd = (t[0].reshape(N_EXPERTS, 1, -1) for t in (b_gate, b_up, b_down))
    sink_rows = jnp.repeat(sink_a[0].astype(F32).reshape(A_KV_HEADS, g).T, A_BLOCK,
                           axis=1).reshape(g, 1, A_KV_HEADS * A_BLOCK)

    nb_p = x_prompt.shape[0]
    mod = _modulation(jnp.concatenate([c_prompt, c_sample], axis=0), ada_w[0], ada_b[0])
    mod = mod.reshape(-1, 6, d)
    s_max = max(x_prompt.shape[1], x_sample.shape[1])
    rope = _rope_tables(s_max)
    bias = _nbr_bias(rpb_b[0])

    def front(x, m, cnt_in):
        qa, ka, va, qb, kb, vb, ga, gb = _in_projection(x, m, g_pre_mix[0], w_in_bf, rope, 512)
        aa = _window_attention(qa, ka, va, sink_rows, 1024)
        ab = _nbr_attention(qb, kb, vb, bias, 1024)
        return _mix_and_route(x, aa, ab, ga, gb, m, wa_bf, wb_bf, wo_bf,
                              g_post_mix[0].reshape(1, d), g_pre_ffn[0].reshape(1, d),
                              wr_bf, br, cnt_in, 2 * TOKEN_BLOCK)

    mod_p, mod_s = mod[:nb_p], mod[nb_p:]
    x1p, h2p, route_p, blk_p, cnt_p = front(x_prompt, mod_p, jnp.zeros((1, LANES), F32))
    x1s, h2s, route_s, blk_s, cnt = front(x_sample, mod_s, cnt_p)

    h2p, h2s = h2p.reshape(-1, d), h2s.reshape(-1, d)
    n_assign = (h2p.shape[0] + h2s.shape[0]) * TOP_K
    ctab, tail, tile_expert, n_used, n_sorted = _route_plan(blk_p, blk_s, cnt, n_assign)

    def by_block(route):
        return route[..., :2 * TOP_K].reshape(-1, TOKEN_BLOCK, 2 * TOP_K).transpose(0, 2, 1)

    rt = jnp.concatenate([by_block(route_p), by_block(route_s)], axis=0)
    xs = _dispatch(ctab, tail, rt, h2p, h2s, n_sorted)
    ys = _grouped_ffn(tile_expert, n_used, xs, wg_bf, bg, wu_bf, bu, wd_bf, bd)
    y_p = _combine(ctab, 0, ys, route_p, x1p, mod_p, g_post_ffn[0])
    y_s = _combine(ctab, h2p.shape[0] // TOKEN_BLOCK, ys, route_s, x1s, mod_s, g_post_ffn[0])
    return (y_p, y_s)
```

```python
import functools

import jax
import jax.numpy as jnp
import numpy as np
from jax import lax
from jax.experimental import pallas as pl
from jax.experimental.pallas import tpu as pltpu

D_MODEL = 1024
HEAD_DIM = 64
ROT_DIM = HEAD_DIM // 4
ROPE_THETA = 500000.0
A_HEADS = 8
A_KV_HEADS = 2
WINDOW = 128
A_BLOCK = 128
B_HEADS = 8
GRID_W = 64
NA_ROWS = 8
NA_COLS = 16
N_EXPERTS = 32
TOP_K = 4
SWIGLU_LIMIT = 7.0
SWIGLU_ALPHA = 1.702
RMS_EPS = 1e-6
NEG = -1e30
A_Q = A_HEADS * HEAD_DIM
A_KV = A_KV_HEADS * HEAD_DIM
B_W = B_HEADS * HEAD_DIM

LANES = 128
PAIR = 2 * HEAD_DIM
N_PAIRS = B_W // PAIR
MOE_TILE = 512
TOKEN_BLOCK = 256
ROW_CHUNK_LOG2 = 3
ROW_CHUNK = 1 << ROW_CHUNK_LOG2
LOCAL_ROWS = TOKEN_BLOCK * TOP_K + N_EXPERTS * ROW_CHUNK
LOCAL_CHUNKS = LOCAL_ROWS // ROW_CHUNK
CTAB_BLOCK = 2 * LANES
DISPATCH_GROUP = 2
IN_FLIGHT_BLOCKS = 2 * DISPATCH_GROUP
VMEM_LIMIT = 56 * 1024 * 1024

BF16 = jnp.bfloat16
F32 = jnp.float32


def _cparams(*sem):
    return pltpu.CompilerParams(dimension_semantics=sem, vmem_limit_bytes=VMEM_LIMIT)


def _mod_body(c_ref, w_ref, b_ref, o_ref):
    c = c_ref[...]
    s = c * (1.0 / (1.0 + jnp.exp(-c)))
    o_ref[...] = jnp.dot(s, w_ref[...], preferred_element_type=F32,
                         precision=lax.Precision.HIGHEST) + b_ref[...]


def _modulation(c, ada_w, ada_b):
    n, d = c.shape
    blk = 1024
    return pl.pallas_call(
        _mod_body,
        grid=(ada_w.shape[1] // blk,),
        in_specs=[pl.BlockSpec((n, d), lambda j: (0, 0)),
                  pl.BlockSpec((d, blk), lambda j: (0, j)),
                  pl.BlockSpec((1, blk), lambda j: (0, j))],
        out_specs=pl.BlockSpec((n, blk), lambda j: (0, j)),
        out_shape=jax.ShapeDtypeStruct((n, ada_w.shape[1]), F32),
        compiler_params=_cparams("arbitrary"),
        name="adaln_mod",
    )(c, ada_w, ada_b.reshape(1, -1))


def _rms(x, g):
    return x * lax.rsqrt(jnp.mean(x * x, axis=-1, keepdims=True) + RMS_EPS) * g


def _rope_tile(seg, rope_ref):
    return (seg * rope_ref[0] + pltpu.roll(seg, LANES - ROT_DIM // 2, 1) * rope_ref[1]
            + pltpu.roll(seg, ROT_DIM // 2, 1) * rope_ref[2])


def _inproj_body(x_ref, mod_ref, g_ref, w_ref, rope_ref,
                 qa_ref, ka_ref, va_ref, qb_ref, kb_ref, vb_ref, ga_ref, gb_ref):
    sh1 = mod_ref[0, 0:1, :]
    sc1 = mod_ref[0, 1:2, :]
    h = (_rms(x_ref[0], g_ref[...]) * (1.0 + sc1) + sh1).astype(BF16)

    def proj(lo, hi):
        return jnp.dot(h, w_ref[:, lo:hi], preferred_element_type=F32)

    qa = proj(0, A_Q)
    for t in range(A_Q // LANES):
        qa_ref[0, :, t * LANES:(t + 1) * LANES] = _rope_tile(
            qa[:, t * LANES:(t + 1) * LANES], rope_ref).astype(BF16)
    o = A_Q
    kva = proj(o, o + 2 * A_KV)
    ka_ref[0] = _rope_tile(kva[:, :A_KV], rope_ref).astype(BF16)
    va_ref[0] = kva[:, A_KV:].astype(BF16)
    o += 2 * A_KV
    for ref in (qb_ref, kb_ref, vb_ref):
        ref[0] = proj(o, o + B_W).astype(BF16)
        o += B_W
    for ref in (ga_ref, gb_ref):
        ref[0] = proj(o, o + D_MODEL).astype(BF16)
        o += D_MODEL


def _in_projection(x, mod, g_pre, w_in_bf, rope, tm):
    b, s, d = x.shape
    widths = (A_Q, A_KV, A_KV, B_W, B_W, B_W, D_MODEL, D_MODEL)
    tok = lambda w: pl.BlockSpec((1, tm, w), lambda i, j: (i, j, 0))
    return pl.pallas_call(
        _inproj_body,
        grid=(b, s // tm),
        in_specs=[tok(d),
                  pl.BlockSpec((1, 6, d), lambda i, j: (i, 0, 0)),
                  pl.BlockSpec((1, d), lambda i, j: (0, 0)),
                  pl.BlockSpec(w_in_bf.shape, lambda i, j: (0, 0)),
                  pl.BlockSpec((3, tm, LANES), lambda i, j: (0, j, 0))],
        out_specs=[tok(w) for w in widths],
        out_shape=[jax.ShapeDtypeStruct((b, s, w), BF16) for w in widths],
        compiler_params=_cparams("arbitrary", "arbitrary"),
        name="in_projection",
    )(x, mod, g_pre.reshape(1, d), w_in_bf, rope)


def _stack_heads(t):
    lane = lax.broadcasted_iota(jnp.int32, t.shape, 1)
    zero = jnp.zeros_like(t)
    return jnp.concatenate([jnp.where(lane < HEAD_DIM, t, zero),
                            jnp.where(lane >= HEAD_DIM, t, zero)], axis=0)


def _window_body(q_ref, k_ref, v_ref, sink_ref, band_ref, o_ref, *, seq, tq):
    span = A_BLOCK + 2 * WINDOW
    lane_o = lax.broadcasted_iota(jnp.int32, (A_BLOCK, LANES), 1)
    for sub in range(tq // A_BLOCK):
        q0 = pl.program_id(1) * tq + sub * A_BLOCK
        k0 = pl.multiple_of(jnp.clip(q0 - WINDOW, 0, seq - span), A_BLOCK)
        kw = k_ref[0, pl.ds(k0, span), :]
        vw = v_ref[0, pl.ds(k0, span), :]
        band = band_ref[lax.div(q0 - k0, A_BLOCK)]
        for pr in range(N_PAIRS):
            qs = _stack_heads(
                q_ref[0, sub * A_BLOCK:(sub + 1) * A_BLOCK, pr * LANES:(pr + 1) * LANES])
            st = lax.dot_general(kw, qs, (((1,), (1,)), ((), ())), preferred_element_type=F32) + band
            snk = sink_ref[pr]
            m = jnp.maximum(jnp.max(st, axis=0, keepdims=True), snk)
            e = jnp.exp(st - m)
            inv = 1.0 / (jnp.sum(e, axis=0, keepdims=True) + jnp.exp(snk - m))
            o = lax.dot_general((e * inv).astype(BF16), vw, (((0,), (0,)), ((), ())),
                                preferred_element_type=F32)
            o_ref[0, sub * A_BLOCK:(sub + 1) * A_BLOCK, pr * LANES:(pr + 1) * LANES] = jnp.where(
                lane_o < HEAD_DIM, o[:A_BLOCK], o[A_BLOCK:]).astype(BF16)


def _window_attention(q, k, v, sink_rows, tq):
    b, s, _ = q.shape
    span = A_BLOCK + 2 * WINDOW
    off = np.arange(3)[:, None, None] * A_BLOCK
    kj = np.arange(span)[None, :, None]
    qi = (np.arange(A_KV_HEADS * A_BLOCK) % A_BLOCK)[None, None, :]
    band = jnp.asarray(np.where(np.abs(qi + off - kj) <= WINDOW, 0.0, NEG).astype(np.float32))
    return pl.pallas_call(
        functools.partial(_window_body, seq=s, tq=tq),
        grid=(b, s // tq),
        in_specs=[pl.BlockSpec((1, tq, A_Q), lambda i, j: (i, j, 0)),
                  pl.BlockSpec((1, s, A_KV), lambda i, j: (i, 0, 0)),
                  pl.BlockSpec((1, s, A_KV), lambda i, j: (i, 0, 0)),
                  pl.BlockSpec(sink_rows.shape, lambda i, j: (0, 0, 0)),
                  pl.BlockSpec(band.shape, lambda i, j: (0, 0, 0))],
        out_specs=pl.BlockSpec((1, tq, A_Q), lambda i, j: (i, j, 0)),
        out_shape=jax.ShapeDtypeStruct((b, s, A_Q), BF16),
        compiler_params=_cparams("arbitrary", "arbitrary"),
        name="window_attention",
    )(q, k, v, sink_rows, band)


def _nbr_bias_body(rpb_ref, o_ref):
    p = pl.program_id(0)
    d = pl.program_id(1)
    nkeys = NA_ROWS * GRID_W
    row = lax.broadcasted_iota(jnp.int32, (2 * GRID_W, nkeys), 0)
    col = lax.broadcasted_iota(jnp.int32, (2 * GRID_W, nkeys), 1)
    qc = row % GRID_W
    kc = col % GRID_W
    qstart = jnp.clip(qc - NA_COLS // 2, 0, GRID_W - NA_COLS)
    valid = (kc >= qstart) & (kc < qstart + NA_COLS)
    cidx = kc - qc + NA_COLS - 1
    key_row = lax.broadcasted_iota(jnp.int32, (1, nkeys), 1) // GRID_W
    acc = jnp.full((2 * GRID_W, nkeys), NEG, F32)
    for j in range(2 * NA_COLS - 1):
        rvs = []
        for hh in range(2):
            rv = jnp.zeros((1, nkeys), F32)
            for aa in range(NA_ROWS):
                rv = jnp.where(key_row == aa, rpb_ref[2 * p + hh, aa - d + NA_ROWS - 1, j], rv)
            rvs.append(rv)
        acc = jnp.where(valid & (cidx == j), jnp.where(row < GRID_W, rvs[0], rvs[1]), acc)
    o_ref[0, 0] = acc.T


def _nbr_bias(rpb):
    nkeys = NA_ROWS * GRID_W
    return pl.pallas_call(
        _nbr_bias_body,
        grid=(N_PAIRS, NA_ROWS),
        in_specs=[pl.BlockSpec(memory_space=pltpu.SMEM)],
        out_specs=pl.BlockSpec((1, 1, nkeys, 2 * GRID_W), lambda p, d: (p, d, 0, 0)),
        out_shape=jax.ShapeDtypeStruct((N_PAIRS, NA_ROWS, nkeys, 2 * GRID_W), F32),
        compiler_params=_cparams("arbitrary", "arbitrary"),
        name="nbr_bias",
    )(rpb)


def _nbr_body(q_ref, k_ref, v_ref, bias_ref, o_ref, *, seq, tq):
    n_rows = seq // GRID_W
    nkeys = NA_ROWS * GRID_W
    lane_o = lax.broadcasted_iota(jnp.int32, (GRID_W, LANES), 1)

    for rl in range(tq // GRID_W):
        r = pl.program_id(1) * (tq // GRID_W) + rl
        r0 = jnp.clip(r - NA_ROWS // 2, 0, n_rows - NA_ROWS)
        d = r - r0
        k0 = pl.multiple_of(r0 * GRID_W, GRID_W)
        q0 = rl * GRID_W
        for pr in range(N_PAIRS):
            lanes = slice(pr * LANES, (pr + 1) * LANES)
            kw = k_ref[0, pl.ds(k0, nkeys), lanes]
            vw = v_ref[0, pl.ds(k0, nkeys), lanes]
            qs = _stack_heads(q_ref[0, pl.ds(q0, GRID_W), lanes])
            st = lax.dot_general(kw, qs, (((1,), (1,)), ((), ())), preferred_element_type=F32)
            st = st + bias_ref[pr, d]
            m = jnp.max(st, axis=0, keepdims=True)
            e = jnp.exp(st - m)
            inv = 1.0 / jnp.sum(e, axis=0, keepdims=True)
            o = lax.dot_general((e * inv).astype(BF16), vw, (((0,), (0,)), ((), ())),
                                preferred_element_type=F32)
            o_ref[0, pl.ds(q0, GRID_W), lanes] = jnp.where(
                lane_o < HEAD_DIM, o[:GRID_W], o[GRID_W:]).astype(BF16)


def _nbr_attention(q, k, v, bias, tq):
    b, s, _ = q.shape
    return pl.pallas_call(
        functools.partial(_nbr_body, seq=s, tq=tq),
        grid=(b, s // tq),
        in_specs=[pl.BlockSpec((1, tq, B_W), lambda i, j: (i, j, 0)),
                  pl.BlockSpec((1, s, B_W), lambda i, j: (i, 0, 0)),
                  pl.BlockSpec((1, s, B_W), lambda i, j: (i, 0, 0)),
                  pl.BlockSpec(bias.shape, lambda i, j: (0, 0, 0, 0))],
        out_specs=pl.BlockSpec((1, tq, B_W), lambda i, j: (i, j, 0)),
        out_shape=jax.ShapeDtypeStruct((b, s, B_W), BF16),
        compiler_params=_cparams("arbitrary", "arbitrary"),
        name="nbr_attention",
    )(q, k, v, bias)


def _sigmoid(z):
    return 1.0 / (1.0 + jnp.exp(-z))


def _mix_body(x_ref, aa_ref, ab_ref, ga_ref, gb_ref, mod_ref, wa_ref, wb_ref, wo_ref,
              gpost_ref, gpre_ref, wr_ref, br_ref, cnt_in_ref,
              x1_ref, h2_ref, route_ref, blk_ref, cnt_ref, run_ref, *, tm):
    first = (pl.program_id(0) == 0) & (pl.program_id(1) == 0)

    @pl.when(first)
    def _():
        run_ref[...] = cnt_in_ref[...]

    ya = jnp.dot(aa_ref[0], wa_ref[...], preferred_element_type=F32)
    yb = jnp.dot(ab_ref[0], wb_ref[...], preferred_element_type=F32)
    merged = _sigmoid(ga_ref[0].astype(F32)) * ya + _sigmoid(gb_ref[0].astype(F32)) * yb
    z = jnp.dot(merged.astype(BF16), wo_ref[...], preferred_element_type=F32)
    gt1 = mod_ref[0, 2:3, :]
    sh2 = mod_ref[0, 3:4, :]
    sc2 = mod_ref[0, 4:5, :]
    x1 = x_ref[0] + gt1 * _rms(z, gpost_ref[...])
    x1_ref[0] = x1
    h2 = _rms(x1, gpre_ref[...]) * (1.0 + sc2) + sh2
    h2_bf = h2.astype(BF16)
    h2_ref[0] = h2_bf

    logits = jnp.dot(h2_bf, wr_ref[...], preferred_element_type=F32) + br_ref[...]
    lane = lax.broadcasted_iota(jnp.int32, (tm, LANES), 1).astype(F32)
    work = logits
    sel = jnp.zeros((tm, LANES), F32)
    vals, idxs = [], []
    for _ in range(TOP_K):
        m = jnp.max(work, axis=-1, keepdims=True)
        idx = jnp.min(jnp.where(work == m, lane, float(LANES)), axis=-1, keepdims=True)
        hit = lane == idx
        vals.append(m)
        idxs.append(idx)
        work = jnp.where(hit, -jnp.inf, work)
        sel = sel + hit.astype(F32)
    es = [jnp.exp(v - vals[0]) for v in vals]
    den = es[0] + es[1] + es[2] + es[3]

    tb = TOKEN_BLOCK
    ri = lax.broadcasted_iota(jnp.int32, (tb, tb), 0)
    ci = lax.broadcasted_iota(jnp.int32, (tb, tb), 1)
    tri = (ci < ri).astype(BF16)
    li = lax.broadcasted_iota(jnp.int32, (LANES, LANES), 0)
    lj = lax.broadcasted_iota(jnp.int32, (LANES, LANES), 1)
    upper = (li < lj).astype(BF16)
    row = lax.broadcasted_iota(jnp.int32, (8, LANES), 0)
    lane_b = lax.broadcasted_iota(jnp.int32, (tb, LANES), 1)
    wts = jnp.zeros((tm, LANES), F32)
    for k in range(TOP_K):
        wts = jnp.where(lane == k, es[k] / den, wts)
    hits = [jnp.where(lane == idxs[k], 1.0, 0.0) for k in range(TOP_K)]
    for sb in range(tm // tb):
        rs = slice(sb * tb, (sb + 1) * tb)
        sel_b = sel[rs]
        earlier = jnp.dot(tri, sel_b.astype(BF16), preferred_element_type=F32)
        n_blk = jnp.sum(sel_b, axis=0, keepdims=True)
        chunks = jnp.floor((n_blk + (ROW_CHUNK - 1.0)) * (1.0 / ROW_CHUNK))
        seg_start = ROW_CHUNK * jnp.dot(jnp.broadcast_to(chunks, (8, LANES)).astype(BF16), upper,
                                        preferred_element_type=F32)[0:1, :]
        slot = earlier + seg_start
        out = wts[rs]
        for k in range(TOP_K):
            local = jnp.sum(hits[k][rs] * slot, axis=-1, keepdims=True)
            out = jnp.where(lane_b == TOP_K + k, local, out)
        route_ref[0, rs, :] = out
        blk_ref[0, sb * 8:(sb + 1) * 8, :] = jnp.where(
            row == 0, n_blk, jnp.where(row == 1, run_ref[...], jnp.where(row == 2, seg_start, 0.0)))
        run_ref[...] = run_ref[...] + ROW_CHUNK * chunks
    cnt_ref[...] = run_ref[...]


def _mix_and_route(x, aa, ab, ga, gb, mod, wa, wb, wo, gpost, gpre, wr, br, cnt_in, tm):
    b, s, d = x.shape
    tok = lambda w: pl.BlockSpec((1, tm, w), lambda i, j: (i, j, 0))
    full = lambda a: pl.BlockSpec(a.shape, lambda i, j: (0,) * a.ndim)
    return pl.pallas_call(
        functools.partial(_mix_body, tm=tm),
        grid=(b, s // tm),
        in_specs=[tok(d), tok(A_Q), tok(B_W), tok(d), tok(d),
                  pl.BlockSpec((1, 6, d), lambda i, j: (i, 0, 0)),
                  full(wa), full(wb), full(wo), full(gpost), full(gpre), full(wr), full(br),
                  full(cnt_in)],
        out_specs=[tok(d), tok(d), tok(LANES),
                   pl.BlockSpec((1, (tm // TOKEN_BLOCK) * 8, LANES), lambda i, j: (i, j, 0)),
                   pl.BlockSpec((1, LANES), lambda i, j: (0, 0))],
        out_shape=[jax.ShapeDtypeStruct((b, s, d), F32), jax.ShapeDtypeStruct((b, s, d), BF16),
                   jax.ShapeDtypeStruct((b, s, LANES), F32),
                   jax.ShapeDtypeStruct((b, (s // TOKEN_BLOCK) * 8, LANES), F32),
                   jax.ShapeDtypeStruct((1, LANES), F32)],
        scratch_shapes=[pltpu.VMEM((1, LANES), F32)],
        compiler_params=_cparams("arbitrary", "arbitrary"),
        name="mix_and_route",
    )(x, aa, ab, ga, gb, mod, wa, wb, wo, gpost, gpre, wr, br, cnt_in)


def _chunk_loop(fn):
    def body(k, carry):
        fn(2 * k, 0)
        fn(2 * k + 1, 1)
        return carry

    lax.fori_loop(0, LOCAL_CHUNKS // 2, body, 0, unroll=4)


def _dispatch_body(ctab_ref, tail_ref, rt_ref, hp_ref, hs_ref, xs_ref, loc_ref, zero_ref, sem,
                   *, np_steps, n_steps):
    i = pl.program_id(0)
    slot = i % 2
    d = D_MODEL
    grp = DISPATCH_GROUP

    def full_wait(s):
        for b in range(grp):
            pltpu.make_async_copy(loc_ref.at[s * grp + b], xs_ref.at[pl.ds(0, LOCAL_ROWS)],
                                  sem.at[s]).wait()

    @pl.when(i == 0)
    def _():
        zero_ref[...] = jnp.zeros_like(zero_ref)

        def chunk_copy(s):
            return pltpu.make_async_copy(zero_ref.at[pl.ds(0, ROW_CHUNK)],
                                         xs_ref.at[pl.ds(s, ROW_CHUNK)], sem.at[0])

        def tails(fn):
            def per_expert(e, carry):
                def chunk(c, carry2):
                    fn(chunk_copy(pl.multiple_of(tail_ref[e] + c * ROW_CHUNK, ROW_CHUNK)))
                    return carry2

                lax.fori_loop(0, tail_ref[N_EXPERTS + e], chunk, 0)
                return carry

            lax.fori_loop(0, N_EXPERTS, per_expert, 0)

        tails(lambda cp: cp.start())
        tails(lambda cp: cp.wait())

        def unused(fn):
            def tile(t, carry):
                r = pl.multiple_of(t * MOE_TILE, MOE_TILE)
                fn(pltpu.make_async_copy(zero_ref, xs_ref.at[pl.ds(r, MOE_TILE)], sem.at[0]))
                return carry

            lax.fori_loop(tail_ref[2 * N_EXPERTS], xs_ref.shape[0] // MOE_TILE, tile, 0)

        unused(lambda cp: cp.start())
        unused(lambda cp: cp.wait())

    @pl.when(i >= 2)
    def _():
        full_wait(slot)

    srow = lax.broadcasted_iota(jnp.int32, (LOCAL_ROWS, TOKEN_BLOCK), 0).astype(F32)
    for b in range(grp):
        buf = slot * grp + b
        rows = slice(b * TOKEN_BLOCK, (b + 1) * TOKEN_BLOCK)
        rt = rt_ref[b]
        hit_any = None
        wsel = jnp.zeros((LOCAL_ROWS, TOKEN_BLOCK), F32)
        for k in range(TOP_K):
            hit = srow == rt[TOP_K + k:TOP_K + k + 1, :]
            wsel = jnp.where(hit, rt[k:k + 1, :], wsel)
            hit_any = hit if hit_any is None else (hit_any | hit)
        onehot = jnp.where(hit_any, 1.0, 0.0).astype(BF16)
        h = jnp.where(i < np_steps, hp_ref[rows, :], hs_ref[rows, :])
        loc_ref[buf, :, 0:d] = jnp.dot(onehot, h, preferred_element_type=F32)
        loc_ref[buf, :, d:d + LANES] = jnp.broadcast_to(jnp.sum(wsel, axis=1, keepdims=True),
                                                        (LOCAL_ROWS, LANES))

    for b in range(grp):
        def start(j, priority, b=b):
            pltpu.make_async_copy(
                loc_ref.at[slot * grp + b, pl.ds(pl.multiple_of(j * ROW_CHUNK, ROW_CHUNK), ROW_CHUNK)],
                xs_ref.at[pl.ds(pl.multiple_of(ctab_ref[b * CTAB_BLOCK + j], ROW_CHUNK), ROW_CHUNK)],
                sem.at[slot]).start(priority=priority)

        _chunk_loop(start)

    @pl.when(i == n_steps - 1)
    def _():
        full_wait(slot)
        if n_steps > 1:
            full_wait(1 - slot)


def _dispatch(ctab, tail, rt, h2p, h2s, n_sorted):
    d = h2p.shape[1]
    grp = DISPATCH_GROUP
    rows = grp * TOKEN_BLOCK
    np_steps = h2p.shape[0] // rows
    ns_steps = h2s.shape[0] // rows
    return pl.pallas_call(
        functools.partial(_dispatch_body, np_steps=np_steps, n_steps=np_steps + ns_steps),
        grid=(np_steps + ns_steps,),
        in_specs=[pl.BlockSpec((grp * CTAB_BLOCK,), lambda i: (i,), memory_space=pltpu.SMEM),
                  pl.BlockSpec((LANES,), lambda i: (0,), memory_space=pltpu.SMEM),
                  pl.BlockSpec((grp, 8, TOKEN_BLOCK), lambda i: (i, 0, 0)),
                  pl.BlockSpec((rows, d), lambda i: (jnp.minimum(i, np_steps - 1), 0)),
                  pl.BlockSpec((rows, d), lambda i: (jnp.clip(i - np_steps, 0, ns_steps - 1), 0))],
        out_specs=pl.BlockSpec(memory_space=pl.ANY),
        out_shape=jax.ShapeDtypeStruct((n_sorted, d + LANES), F32),
        scratch_shapes=[pltpu.VMEM((2 * grp, LOCAL_ROWS, d + LANES), F32),
                        pltpu.VMEM((MOE_TILE, d + LANES), F32),
                        pltpu.SemaphoreType.DMA((2,))],
        compiler_params=_cparams("arbitrary"),
        name="moe_dispatch",
    )(ctab, tail, rt, h2p, h2s)


def _gmm_body(te_ref, nu_ref, x_ref, wg_ref, bg_ref, wu_ref, bu_ref, wd_ref, bd_ref, y_ref,
              wg_bf, wu_bf, wd_bf):
    i = pl.program_id(0)
    d = D_MODEL
    used = i < nu_ref[0]

    @pl.when(used & ((i == 0) | (te_ref[i] != te_ref[jnp.maximum(i - 1, 0)])))
    def _():
        wg_bf[...] = wg_ref[0].astype(BF16)
        wu_bf[...] = wu_ref[0].astype(BF16)
        wd_bf[...] = wd_ref[0].astype(BF16)

    @pl.when(used)
    def _():
        x = x_ref[:, 0:d].astype(BF16)
        g = jnp.minimum(jnp.dot(x, wg_bf[...], preferred_element_type=F32) + bg_ref[0], SWIGLU_LIMIT)
        u = jnp.clip(jnp.dot(x, wu_bf[...], preferred_element_type=F32) + bu_ref[0],
                     -SWIGLU_LIMIT, SWIGLU_LIMIT)
        act = (u + 1.0) * (g * _sigmoid(SWIGLU_ALPHA * g))
        y = jnp.dot(act.astype(BF16), wd_bf[...], preferred_element_type=F32) + bd_ref[0]
        y_ref[...] = x_ref[:, d:d + 1] * y

    @pl.when(i >= nu_ref[0])
    def _():
        y_ref[...] = jnp.zeros_like(y_ref)


def _grouped_ffn(tile_expert, n_used, xs, wg, bg, wu, bu, wd, bd):
    n = xs.shape[0]
    d, f = wg.shape[1], wg.shape[2]
    wspec = lambda a, b_: pl.BlockSpec((1, a, b_), lambda i, te, nu: (te[i], 0, 0))
    return pl.pallas_call(
        _gmm_body,
        grid_spec=pltpu.PrefetchScalarGridSpec(
            num_scalar_prefetch=2,
            grid=(n // MOE_TILE,),
            in_specs=[pl.BlockSpec((MOE_TILE, d + LANES),
                                   lambda i, te, nu: (jnp.minimum(i, nu[0] - 1), 0)),
                      wspec(d, f), wspec(1, f), wspec(d, f), wspec(1, f), wspec(f, d), wspec(1, d)],
            out_specs=pl.BlockSpec((MOE_TILE, d), lambda i, te, nu: (i, 0)),
            scratch_shapes=[pltpu.VMEM((d, f), BF16), pltpu.VMEM((d, f), BF16),
                            pltpu.VMEM((f, d), BF16)],
        ),
        out_shape=jax.ShapeDtypeStruct((n, d), F32),
        compiler_params=_cparams("arbitrary"),
        name="moe_grouped_ffn",
    )(tile_expert, n_used, xs, wg, bg, wu, bu, wd, bd)


def _combine_body(ctab_ref, cnext_ref, ys_ref, route_ref, x1_ref, mod_ref, g_ref, o_ref, loc_ref, sem,
                  *, n_steps):
    t = pl.program_id(0) * pl.num_programs(1) + pl.program_id(1)
    slot = t % 2
    grp = DISPATCH_GROUP

    def fetch(tab_ref, s):
        for b in range(grp):
            def start(j, priority, b=b):
                pltpu.make_async_copy(
                    ys_ref.at[pl.ds(pl.multiple_of(tab_ref[b * CTAB_BLOCK + j], ROW_CHUNK), ROW_CHUNK)],
                    loc_ref.at[s * grp + b, pl.ds(pl.multiple_of(j * ROW_CHUNK, ROW_CHUNK), ROW_CHUNK)],
                    sem.at[s]).start(priority=priority)

            _chunk_loop(start)

    @pl.when(t == 0)
    def _():
        fetch(ctab_ref, slot)

    @pl.when(t + 1 < n_steps)
    def _():
        fetch(cnext_ref, 1 - slot)

    for b in range(grp):
        pltpu.make_async_copy(ys_ref.at[pl.ds(0, LOCAL_ROWS)], loc_ref.at[slot * grp + b],
                              sem.at[slot]).wait()

    gt2 = mod_ref[0, 5:6, :]
    col = lax.broadcasted_iota(jnp.int32, (TOKEN_BLOCK, LOCAL_ROWS), 1).astype(F32)
    for b in range(grp):
        rows = slice(b * TOKEN_BLOCK, (b + 1) * TOKEN_BLOCK)
        route = route_ref[0, rows, :]
        hit_any = None
        for k in range(TOP_K):
            hit = col == route[:, TOP_K + k:TOP_K + k + 1]
            hit_any = hit if hit_any is None else (hit_any | hit)
        onehot = jnp.where(hit_any, 1.0, 0.0).astype(BF16)
        y2 = jnp.dot(onehot, loc_ref[slot * grp + b].astype(BF16), preferred_element_type=F32)
        o_ref[0, rows, :] = x1_ref[0, rows, :] + gt2 * _rms(y2, g_ref[...])


def _combine(ctab, first_block, ys, route, x1, mod, g_post):
    b, s, d = x1.shape
    grp = DISPATCH_GROUP
    rows = grp * TOKEN_BLOCK
    spb = s // rows
    n_steps = b * spb
    first = first_block // grp
    tok = lambda w: pl.BlockSpec((1, rows, w), lambda i, j: (i, j, 0))
    step = lambda i, j: first + i * spb + j
    return pl.pallas_call(
        functools.partial(_combine_body, n_steps=n_steps),
        grid=(b, spb),
        in_specs=[pl.BlockSpec((grp * CTAB_BLOCK,), lambda i, j: (step(i, j),),
                               memory_space=pltpu.SMEM),
                  pl.BlockSpec((grp * CTAB_BLOCK,),
                               lambda i, j: (jnp.minimum(step(i, j) + 1, first + n_steps - 1),),
                               memory_space=pltpu.SMEM),
                  pl.BlockSpec(memory_space=pl.ANY),
                  tok(LANES), tok(d),
                  pl.BlockSpec((1, 6, d), lambda i, j: (i, 0, 0)),
                  pl.BlockSpec((1, d), lambda i, j: (0, 0))],
        out_specs=tok(d),
        out_shape=jax.ShapeDtypeStruct((b, s, d), F32),
        scratch_shapes=[pltpu.VMEM((2 * grp, LOCAL_ROWS, d), F32), pltpu.SemaphoreType.DMA((2,))],
        compiler_params=_cparams("arbitrary", "arbitrary"),
        name="moe_combine",
    )(ctab, ctab, ys, route, x1, mod, g_post.reshape(1, d))


def _rope_tables(s_max):
    half = ROT_DIM // 2
    inv = jnp.power(jnp.float32(ROPE_THETA), -jnp.arange(0, ROT_DIM, 2, dtype=F32) / ROT_DIM)
    ang = jnp.arange(s_max, dtype=F32)[:, None] * inv[None, :]
    cos, sin = jnp.cos(ang), jnp.sin(ang)
    d = np.arange(LANES) % HEAD_DIM
    lo = jnp.asarray(d < half)
    hi = jnp.asarray((d >= half) & (d < ROT_DIM))
    cos_l = cos[:, d % half]
    sin_l = sin[:, d % half]
    c = jnp.where(lo | hi, cos_l, 1.0)
    s1 = jnp.where(lo, -sin_l, 0.0)
    s2 = jnp.where(hi, sin_l, 0.0)
    return jnp.stack([c, s1, s2]).astype(F32)


def _route_plan(blk_p, blk_s, counts, n_assign):
    blk = jnp.concatenate([blk_p.reshape(-1, 8, LANES), blk_s.reshape(-1, 8, LANES)], axis=0)
    n_blk = blk[:, 0, :N_EXPERTS].astype(jnp.int32)
    before = blk[:, 1, :N_EXPERTS].astype(jnp.int32)
    cnt = counts[0, :N_EXPERTS].astype(jnp.int32)
    tiles = (cnt + (MOE_TILE - 1)) // MOE_TILE
    tile_end = jnp.cumsum(tiles)
    start = (tile_end - tiles) * MOE_TILE
    n_pad = blk.shape[0] * N_EXPERTS * (ROW_CHUNK - 1)
    n_bound = (n_assign + n_pad) // MOE_TILE + N_EXPERTS
    n_tiles = n_bound + (IN_FLIGHT_BLOCKS * LOCAL_ROWS + MOE_TILE - 1) // MOE_TILE
    experts = jnp.arange(N_EXPERTS, dtype=jnp.int32)
    chunks = (n_blk + (ROW_CHUNK - 1)) // ROW_CHUNK
    chunk_end = jnp.cumsum(chunks, axis=1)
    j = jnp.arange(LOCAL_CHUNKS, dtype=jnp.int32)
    e = jnp.sum(chunk_end[:, None, :] <= j[None, :, None], axis=-1)
    pick = lambda table: jnp.sum(jnp.where(e[..., None] == experts, table[:, None, :], 0), axis=-1)
    in_segment = pick(start[None, :] + before) + ROW_CHUNK * (j[None, :] - pick(chunk_end - chunks))
    parity = (jnp.arange(blk.shape[0], dtype=jnp.int32) % IN_FLIGHT_BLOCKS)[:, None]
    scratch_row = n_bound * MOE_TILE + parity * LOCAL_ROWS + ROW_CHUNK * j[None, :]
    ctab = jnp.where(e < N_EXPERTS, in_segment, scratch_row)
    ctab = jnp.pad(ctab, ((0, 0), (0, CTAB_BLOCK - LOCAL_CHUNKS))).reshape(-1).astype(jnp.int32)
    tile_id = jnp.arange(n_tiles, dtype=jnp.int32)
    tile_expert = jnp.minimum(jnp.sum(tile_end[None, :] <= tile_id[:, None], axis=-1),
                              N_EXPERTS - 1).astype(jnp.int32)
    n_used = tile_end[-1:].astype(jnp.int32)
    zero0 = start + cnt
    zero_chunks = (tile_end * MOE_TILE - zero0) // ROW_CHUNK
    tail = jnp.concatenate([zero0, zero_chunks, n_used,
                            jnp.zeros((LANES - 2 * N_EXPERTS - 1,), jnp.int32)])
    return ctab, tail.astype(jnp.int32), tile_expert, n_used, n_tiles * MOE_TILE


def kernel(x_prompt, x_sample, c_prompt, c_sample, ada_w, ada_b, g_pre_mix, g_post_mix, g_pre_ffn,
           g_post_ffn, w_in, sink_a, rpb_b, w_branch_a, w_branch_b, w_out, w_router, b_router,
           w_gate, b_gate, w_up, b_up, w_down, b_down):
    d = D_MODEL
    scale = HEAD_DIM ** -0.5
    g = A_HEADS // A_KV_HEADS
    w = w_in[0]
    col_scale = np.ones((w.shape[1],), np.float32)
    col_scale[:A_Q] = scale
    qb0 = A_Q + 2 * A_KV
    col_scale[qb0:qb0 + B_W] = scale
    wq = w[:, :A_Q].reshape(d, A_KV_HEADS, g, HEAD_DIM).transpose(0, 2, 1, 3).reshape(d, A_Q)
    w_in_bf = (jnp.concatenate([wq, w[:, A_Q:]], axis=1) * col_scale).astype(BF16)
    wa_bf = w_branch_a[0].reshape(A_KV_HEADS, g, HEAD_DIM, d).transpose(1, 0, 2, 3).reshape(
        A_Q, d).astype(BF16)
    wb_bf = w_branch_b[0].astype(BF16)
    wo_bf = w_out[0].astype(BF16)
    wr_bf = jnp.zeros((d, LANES), F32).at[:, :N_EXPERTS].set(w_router[0]).astype(BF16)
    br = jnp.full((1, LANES), NEG, F32).at[0, :N_EXPERTS].set(b_router[0])
    wg_bf, wu_bf, wd_bf = w_gate[0], w_up[0], w_down[0]
    bg, bu, bd = (t[0].reshape(N_EXPERTS, 1, -1) for t in (b_gate, b_up, b_down))
    sink_rows = jnp.repeat(sink_a[0].astype(F32).reshape(A_KV_HEADS, g).T, A_BLOCK,
                           axis=1).reshape(g, 1, A_KV_HEADS * A_BLOCK)

    nb_p = x_prompt.shape[0]
    mod = _modulation(jnp.concatenate([c_prompt, c_sample], axis=0), ada_w[0], ada_b[0])
    mod = mod.reshape(-1, 6, d)
    s_max = max(x_prompt.shape[1], x_sample.shape[1])
    rope = _rope_tables(s_max)
    bias = _nbr_bias(rpb_b[0])

    def front(x, m, cnt_in):
        qa, ka, va, qb, kb, vb, ga, gb = _in_projection(x, m, g_pre_mix[0], w_in_bf, rope, 512)
        aa = _window_attention(qa, ka, va, sink_rows, 1024)
        ab = _nbr_attention(qb, kb, vb, bias, 1024)
        return _mix_and_route(x, aa, ab, ga, gb, m, wa_bf, wb_bf, wo_bf,
                              g_post_mix[0].reshape(1, d), g_pre_ffn[0].reshape(1, d),
                              wr_bf, br, cnt_in, 2 * TOKEN_BLOCK)

    mod_p, mod_s = mod[:nb_p], mod[nb_p:]
    x1p, h2p, route_p, blk_p, cnt_p = front(x_prompt, mod_p, jnp.zeros((1, LANES), F32))
    x1s, h2s, route_s, blk_s, cnt = front(x_sample, mod_s, cnt_p)

    h2p, h2s = h2p.reshape(-1, d), h2s.reshape(-1, d)
    n_assign = (h2p.shape[0] + h2s.shape[0]) * TOP_K
    ctab, tail, tile_expert, n_used, n_sorted = _route_plan(blk_p, blk_s, cnt, n_assign)

    def by_block(route):
        return route[..., :2 * TOP_K].reshape(-1, TOKEN_BLOCK, 2 * TOP_K).transpose(0, 2, 1)

    rt = jnp.concatenate([by_block(route_p), by_block(route_s)], axis=0)
    xs = _dispatch(ctab, tail, rt, h2p, h2s, n_sorted)
    ys = _grouped_ffn(tile_expert, n_used, xs, wg_bf, bg, wu_bf, bu, wd_bf, bd)
    y_p = _combine(ctab, 0, ys, route_p, x1p, mod_p, g_post_ffn[0])
    y_s = _combine(ctab, h2p.shape[0] // TOKEN_BLOCK, ys, route_s, x1s, mod_s, g_post_ffn[0])
    return (y_p, y_s)
```
